```python
import jax, jax.numpy as jnp
from jax import lax
import numpy as np

D_MODEL = 1024
BATCH = 4
SEQ = 8192
DEPTH = 1

GRID_W = 64
CTX_LEN = 256
RW_HEADS = 8
RW_HEAD_DIM = 64
RW_WIDTH = RW_HEADS * RW_HEAD_DIM
RW_DECAY_LORA = 64
RW_AAA_LORA = 64
RW_GATE_LORA = 128
MLA_HEADS = 4
MLA_Q_LORA = 256
MLA_KV_LORA = 256
MLA_NOPE_DIM = 128
MLA_ROPE_DIM = 64
MLA_V_DIM = 128
MLA_WIDTH = MLA_HEADS * MLA_V_DIM
MIX_WIDTH = RW_WIDTH + MLA_WIDTH
ROPE_BASE = 10000.0
Q_BLOCK = 128
N_EXPERTS = 16
EXPERT_HIDDEN = 1024
CAPACITY_FACTOR = 2
LN_EPS = 1e-5
RMS_EPS = 1e-6
GN_EPS = 64e-5
RW_PROJ = 3 * RW_WIDTH + 2 * RW_DECAY_LORA + 2 * RW_AAA_LORA + RW_GATE_LORA
MLA_PROJ = MLA_Q_LORA + MLA_KV_LORA + MLA_ROPE_DIM
IN_PROJ = RW_PROJ + MLA_PROJ

kernel_name = 'hybrid_rwkv7_mla_ecmoe_dit_block'


def _standardize(x, eps):
    xf = x.astype(jnp.float32)
    mu = jnp.mean(xf, axis=-1, keepdims=True)
    var = jnp.mean(jnp.square(xf - mu), axis=-1, keepdims=True)
    return (xf - mu) * lax.rsqrt(var + eps)


def modulate(x, shift, scale):
    return (_standardize(x, LN_EPS) * (1.0 + scale) + shift).astype(x.dtype)


def layer_norm(x, g, b):
    return (_standardize(x, LN_EPS) * g + b).astype(x.dtype)


def rms_norm(x, g):
    xf = x.astype(jnp.float32)
    return (xf * lax.rsqrt(jnp.mean(jnp.square(xf), -1, keepdims=True) + RMS_EPS) * g).astype(x.dtype)


def split_cols(h, sizes):
    return jnp.split(h, np.cumsum(sizes)[:-1].tolist(), axis=-1)


def centred_conv3(h, w):
    hp = jnp.pad(h, ((0, 0), (1, 1), (0, 0)))
    return w[0] * hp[:, :-2] + w[1] * hp[:, 1:-1] + w[2] * hp[:, 2:]


def axial_angles(rows):
    rr, cc = jnp.meshgrid(jnp.arange(rows), jnp.arange(GRID_W), indexing='ij')
    half = MLA_ROPE_DIM // 2
    inv_freq = ROPE_BASE ** (-jnp.arange(0, half, 2, dtype=jnp.float32) / half)
    ang_r = rr.reshape(-1, 1).astype(jnp.float32) * inv_freq
    ang_c = cc.reshape(-1, 1).astype(jnp.float32) * inv_freq
    return ang_r, ang_c


def rotate_pairs(x, ang):
    x1, x2 = jnp.split(x, 2, axis=-1)
    cos = jnp.cos(ang).astype(x.dtype)
    sin = jnp.sin(ang).astype(x.dtype)
    return jnp.concatenate([x1 * cos - x2 * sin, x2 * cos + x1 * sin], axis=-1)


def rope_axial(x, ang_r, ang_c):
    half = MLA_ROPE_DIM // 2
    return jnp.concatenate([rotate_pairs(x[..., :half], ang_r), rotate_pairs(x[..., half:], ang_c)], axis=-1)


def rwkv_prepare(h, p):
    B, T, _ = h.shape
    heads = lambda t: t.reshape(B, T, RW_HEADS, RW_HEAD_DIM).astype(jnp.float32)
    rkv, w_dn, a_dn, g_dn = split_cols(h, (3 * RW_WIDTH, 2 * RW_DECAY_LORA, 2 * RW_AAA_LORA, RW_GATE_LORA))
    r, k, v = jnp.split(centred_conv3(rkv, p['rwkv_conv']), 3, axis=-1)
    g = jax.nn.sigmoid(g_dn) @ p['rwkv_g_up']
    kk = heads(k * p['rwkv_k_k'])
    kk = kk * lax.rsqrt(jnp.sum(jnp.square(kk), -1, keepdims=True) + 1e-12)
    w_dirs = jnp.split(w_dn, 2, axis=-1)
    a_dirs = jnp.split(a_dn, 2, axis=-1)
    per_dir = []
    for d in range(2):
        logw = -jax.nn.softplus(-(p['rwkv_w0'][d] + jnp.tanh(w_dirs[d]) @ p['rwkv_w_up'][d]).astype(jnp.float32)) - 0.5
        decay = jnp.exp(-jnp.exp(logw))
        a = jax.nn.sigmoid(p['rwkv_a0'][d] + a_dirs[d] @ p['rwkv_a_up'][d])
        k_d = k * (1.0 + (a - 1.0) * p['rwkv_k_a'])
        per_dir.append((heads(decay), heads(a), heads(k_d)))
    return heads(r), heads(v), kk, g, per_dir


def rwkv_scan(r, decay, k, v, kk, a, state0, reverse, emit):
    def step(S, inp):
        r_t, w_t, k_t, v_t, kk_t, a_t = inp
        s_kk = jnp.einsum('bhvk,bhk->bhv', S, kk_t)
        S = S * w_t[:, :, None, :] - s_kk[..., None] * (kk_t * a_t)[:, :, None, :] + v_t[..., None] * k_t[:, :, None, :]
        y = jnp.einsum('bhvk,bhk->bhv', S, r_t) if emit else None
        return S, y
    xs = tuple(jnp.moveaxis(t, 1, 0) for t in (r, decay, k, v, kk, a))
    state, ys = lax.scan(step, state0, xs, reverse=reverse)
    return state, (jnp.moveaxis(ys, 0, 1) if emit else None)


def rwkv_readout(y, r, v, k_sum, g, p):
    B, T = y.shape[:2]
    yn = _standardize(y, GN_EPS).reshape(B, T, RW_WIDTH) * p['rwkv_gn_g'] + p['rwkv_gn_b']
    bonus = jnp.sum(r * k_sum * p['rwkv_r_k'], -1, keepdims=True) * v
    return (yn + bonus.reshape(B, T, RW_WIDTH)).astype(g.dtype) * g


def rwkv_group(h, hc, p, emit_ctx):
    r, v, kk, g, dirs = rwkv_prepare(h, p)
    rc, vc, kkc, gc, dirs_c = rwkv_prepare(hc, p)
    B = h.shape[0]
    ys_lat, ys_ctx = [], []
    for d in range(2):
        decay, a, k_d = dirs[d]
        decay_c, a_c, k_dc = dirs_c[d]
        state0 = jnp.zeros((B, RW_HEADS, RW_HEAD_DIM, RW_HEAD_DIM), jnp.float32)
        state_c, yc = rwkv_scan(rc, decay_c, k_dc, vc, kkc, a_c, state0, d == 1, emit_ctx)
        _, yl = rwkv_scan(r, decay, k_d, v, kk, a, state_c, d == 1, True)
        ys_lat.append(yl)
        ys_ctx.append(yc)
    out_lat = rwkv_readout(ys_lat[0] + ys_lat[1], r, v, dirs[0][2] + dirs[1][2], g, p)
    out_ctx = None
    if emit_ctx:
        out_ctx = rwkv_readout(ys_ctx[0] + ys_ctx[1], rc, vc, dirs_c[0][2] + dirs_c[1][2], gc, p)
    return out_lat, out_ctx


def mla_queries(h, p, ang):
    B, T, _ = h.shape
    q = rms_norm(h[..., :MLA_Q_LORA], p['mla_q_norm']) @ p['mla_w_uq']
    q = q.reshape(B, T, MLA_HEADS, MLA_NOPE_DIM + MLA_ROPE_DIM)
    q_nope, q_rope = q[..., :MLA_NOPE_DIM], q[..., MLA_NOPE_DIM:]
    if ang is not None:
        q_rope = rope_axial(q_rope, ang[0][:, None, :], ang[1][:, None, :])
    return q_nope, q_rope


def mla_keys_values(h, p, ang):
    B, T, _ = h.shape
    c_kv = rms_norm(h[..., MLA_Q_LORA:MLA_Q_LORA + MLA_KV_LORA], p['mla_kv_norm'])
    k_rope = h[..., MLA_Q_LORA + MLA_KV_LORA:]
    if ang is not None:
        k_rope = rope_axial(k_rope, ang[0], ang[1])
    k_nope = (c_kv @ p['mla_w_uk']).reshape(B, T, MLA_HEADS, MLA_NOPE_DIM)
    v = (c_kv @ p['mla_w_uv']).reshape(B, T, MLA_HEADS, MLA_V_DIM)
    return k_nope, k_rope, v


def block_attention(q_nope, q_rope, k_nope, k_rope, v):
    B, T, H, _ = q_nope.shape
    nb = T // Q_BLOCK
    scale = (MLA_NOPE_DIM + MLA_ROPE_DIM) ** -0.5
    qn = q_nope.reshape(B, nb, Q_BLOCK, H, MLA_NOPE_DIM).transpose(1, 0, 2, 3, 4)
    qr = q_rope.reshape(B, nb, Q_BLOCK, H, MLA_ROPE_DIM).transpose(1, 0, 2, 3, 4)

    def one_block(args):
        qn_b, qr_b = args
        s = jnp.einsum('bqhd,bkhd->bhqk', qn_b, k_nope) + jnp.einsum('bqhr,bkr->bhqk', qr_b, k_rope)
        prob = jax.nn.softmax(s.astype(jnp.float32) * scale, axis=-1).astype(v.dtype)
        return jnp.einsum('bhqk,bkhd->bqhd', prob, v)

    o = lax.map(one_block, (qn, qr))
    return o.transpose(1, 0, 2, 3, 4).reshape(B, T, H * MLA_V_DIM)


def mla_group(h, hc, p, ang, emit_ctx):
    kc_nope, kc_rope, vc = mla_keys_values(hc, p, None)
    k_nope, k_rope, v = mla_keys_values(h, p, ang)
    q_nope, q_rope = mla_queries(h, p, ang)
    out_lat = block_attention(q_nope, q_rope,
                              jnp.concatenate([kc_nope, k_nope], axis=1),
                              jnp.concatenate([kc_rope, k_rope], axis=1),
                              jnp.concatenate([vc, v], axis=1))
    out_ctx = None
    if emit_ctx:
        qc_nope, qc_rope = mla_queries(hc, p, None)
        out_ctx = block_attention(qc_nope, qc_rope, kc_nope, kc_rope, vc)
    return out_lat, out_ctx


def expert_choice_ffn(u, p):
    B, T, _ = u.shape
    cap = CAPACITY_FACTOR * T // N_EXPERTS
    affinity = jax.nn.softmax((u @ p['router']).astype(jnp.float32), axis=-1)
    gate, idx = lax.top_k(jnp.swapaxes(affinity, 1, 2), cap)
    b_idx = jnp.arange(B)[:, None, None]
    xe = u[b_idx, idx]
    hid = jax.nn.silu(jnp.einsum('becd,edf->becf', xe, p['exp_w_gate'])) * jnp.einsum('becd,edf->becf', xe, p['exp_w_up'])
    ye = jnp.einsum('becf,efd->becd', hid, p['exp_w_down']) * gate[..., None].astype(u.dtype)
    return jnp.zeros_like(u).at[b_idx, idx].add(ye)


def hybrid_layer(x, xc, mod, mod_c, ang, p, alpha, emit_ctx):
    sh1, sc1, g1, sh2, sc2, g2 = jnp.split(mod, 6, axis=-1)
    sh1c, sc1c, g1c, sh2c, sc2c, g2c = jnp.split(mod_c, 6, axis=-1)
    h = modulate(x, sh1, sc1) @ p['w_in']
    hc = modulate(xc, sh1c, sc1c) @ p['w_in']
    rw_lat, rw_ctx = rwkv_group(h[..., :RW_PROJ], hc[..., :RW_PROJ], p, emit_ctx)
    mla_lat, mla_ctx = mla_group(h[..., RW_PROJ:], hc[..., RW_PROJ:], p, ang, emit_ctx)
    mix = jnp.concatenate([rw_lat, mla_lat], axis=-1) @ p['w_out']
    x = layer_norm(alpha * x + g1 * mix, p['ln1_g'], p['ln1_b'])
    x = layer_norm(alpha * x + g2 * expert_choice_ffn(modulate(x, sh2, sc2), p), p['ln2_g'], p['ln2_b'])
    if emit_ctx:
        mix_c = jnp.concatenate([rw_ctx, mla_ctx], axis=-1) @ p['w_out']
        xc = layer_norm(alpha * xc + g1c * mix_c, p['ln1_g'], p['ln1_b'])
        xc = layer_norm(alpha * xc + g2c * expert_choice_ffn(modulate(xc, sh2c, sc2c), p), p['ln2_g'], p['ln2_b'])
    return x, xc


def setup_inputs(seed: int = 0) -> dict:
    key = jax.random.key(seed)
    keys = list(jax.random.split(key, 40))
    nk = lambda: keys.pop()
    f32 = jnp.float32
    normal = lambda shape: jax.random.normal(nk(), shape, f32)
    nrm = lambda shape, fan_in, gain=1.0: normal(shape) * (gain * fan_in ** -0.5)
    L, D = DEPTH, D_MODEL
    beta = (8.0 * DEPTH) ** -0.25
    x = normal((BATCH, SEQ, D))
    c = normal((BATCH, D))
    ctx = normal((BATCH, CTX_LEN, D))
    c_ctx = normal((D,))
    w_ada = nrm((L, D, 6 * D), D, 0.5)
    b_ada = 0.02 * normal((L, 6 * D))
    w_in = nrm((L, D, IN_PROJ), D)
    w_in = w_in.at[:, :, 2 * RW_WIDTH:3 * RW_WIDTH].multiply(beta)
    rwkv_conv = jnp.array([0.25, 1.0, 0.25], f32)[None, :, None] + 0.05 * normal((L, 3, 3 * RW_WIDTH))
    decay_ramp = jnp.tile(jnp.linspace(-6.5, -1.0, RW_HEAD_DIM, dtype=f32), RW_HEADS)
    rwkv_w0 = decay_ramp + 0.1 * normal((L, 2, RW_WIDTH))
    rwkv_w_up = 0.1 * normal((L, 2, RW_DECAY_LORA, RW_WIDTH))
    rwkv_a0 = 0.1 * normal((L, 2, RW_WIDTH))
    rwkv_a_up = nrm((L, 2, RW_AAA_LORA, RW_WIDTH), RW_AAA_LORA, 0.5)
    rwkv_g_up = nrm((L, RW_GATE_LORA, RW_WIDTH), RW_GATE_LORA)
    rwkv_k_k = 0.85 + 0.05 * normal((L, RW_WIDTH))
    rwkv_k_a = 1.0 + 0.05 * normal((L, RW_WIDTH))
    rwkv_r_k = 0.1 * normal((L, RW_HEADS, RW_HEAD_DIM))
    rwkv_gn_g = 1.0 + 0.05 * normal((L, RW_WIDTH))
    rwkv_gn_b = 0.02 * normal((L, RW_WIDTH))
    mla_q_norm = 1.0 + 0.05 * normal((L, MLA_Q_LORA))
    mla_w_uq = nrm((L, MLA_Q_LORA, MLA_HEADS * (MLA_NOPE_DIM + MLA_ROPE_DIM)), MLA_Q_LORA)
    mla_kv_norm = 1.0 + 0.05 * normal((L, MLA_KV_LORA))
    mla_w_uk = nrm((L, MLA_KV_LORA, MLA_HEADS * MLA_NOPE_DIM), MLA_KV_LORA)
    mla_w_uv = nrm((L, MLA_KV_LORA, MLA_HEADS * MLA_V_DIM), MLA_KV_LORA, beta)
    w_out = nrm((L, MIX_WIDTH, D), MIX_WIDTH, beta)
    ln1_g = 1.0 + 0.05 * normal((L, D))
    ln1_b = 0.02 * normal((L, D))
    router = nrm((L, D, N_EXPERTS), D)
    exp_w_gate = nrm((L, N_EXPERTS, D, EXPERT_HIDDEN), D)
    exp_w_up = nrm((L, N_EXPERTS, D, EXPERT_HIDDEN), D)
    exp_w_down = nrm((L, N_EXPERTS, EXPERT_HIDDEN, D), EXPERT_HIDDEN, beta)
    ln2_g = 1.0 + 0.05 * normal((L, D))
    ln2_b = 0.02 * normal((L, D))
    return {'x': x, 'c': c, 'ctx': ctx, 'c_ctx': c_ctx, 'w_ada': w_ada, 'b_ada': b_ada, 'w_in': w_in,
            'rwkv_conv': rwkv_conv, 'rwkv_w0': rwkv_w0, 'rwkv_w_up': rwkv_w_up, 'rwkv_a0': rwkv_a0,
            'rwkv_a_up': rwkv_a_up, 'rwkv_g_up': rwkv_g_up, 'rwkv_k_k': rwkv_k_k, 'rwkv_k_a': rwkv_k_a,
            'rwkv_r_k': rwkv_r_k, 'rwkv_gn_g': rwkv_gn_g, 'rwkv_gn_b': rwkv_gn_b, 'mla_q_norm': mla_q_norm,
            'mla_w_uq': mla_w_uq, 'mla_kv_norm': mla_kv_norm, 'mla_w_uk': mla_w_uk, 'mla_w_uv': mla_w_uv,
            'w_out': w_out, 'ln1_g': ln1_g, 'ln1_b': ln1_b, 'router': router, 'exp_w_gate': exp_w_gate,
            'exp_w_up': exp_w_up, 'exp_w_down': exp_w_down, 'ln2_g': ln2_g, 'ln2_b': ln2_b}


def reference(x, c, ctx, c_ctx, w_ada, b_ada, w_in, rwkv_conv, rwkv_w0, rwkv_w_up, rwkv_a0, rwkv_a_up,
              rwkv_g_up, rwkv_k_k, rwkv_k_a, rwkv_r_k, rwkv_gn_g, rwkv_gn_b, mla_q_norm, mla_w_uq,
              mla_kv_norm, mla_w_uk, mla_w_uv, w_out, ln1_g, ln1_b, router, exp_w_gate, exp_w_up,
              exp_w_down, ln2_g, ln2_b):
    ROWS = x.shape[1] // GRID_W
    ang = axial_angles(ROWS)
    alpha = (2.0 * DEPTH) ** 0.25
    for l in range(DEPTH):
        p = {'w_in': w_in[l], 'rwkv_conv': rwkv_conv[l], 'rwkv_w0': rwkv_w0[l], 'rwkv_w_up': rwkv_w_up[l],
             'rwkv_a0': rwkv_a0[l], 'rwkv_a_up': rwkv_a_up[l], 'rwkv_g_up': rwkv_g_up[l],
             'rwkv_k_k': rwkv_k_k[l], 'rwkv_k_a': rwkv_k_a[l], 'rwkv_r_k': rwkv_r_k[l],
             'rwkv_gn_g': rwkv_gn_g[l], 'rwkv_gn_b': rwkv_gn_b[l], 'mla_q_norm': mla_q_norm[l],
             'mla_w_uq': mla_w_uq[l], 'mla_kv_norm': mla_kv_norm[l], 'mla_w_uk': mla_w_uk[l],
             'mla_w_uv': mla_w_uv[l], 'w_out': w_out[l], 'ln1_g': ln1_g[l], 'ln1_b': ln1_b[l],
             'router': router[l], 'exp_w_gate': exp_w_gate[l], 'exp_w_up': exp_w_up[l],
             'exp_w_down': exp_w_down[l], 'ln2_g': ln2_g[l], 'ln2_b': ln2_b[l]}
        mod = jax.nn.silu(c) @ w_ada[l] + b_ada[l]
        mod_c = jax.nn.silu(c_ctx) @ w_ada[l] + b_ada[l]
        x, ctx = hybrid_layer(x, ctx, mod[:, None, :], mod_c, ang, p, alpha, l < DEPTH - 1)
    return x
```

```python
import functools

import jax
import jax.numpy as jnp
import numpy as np
from jax import lax
from jax.experimental import pallas as pl
from jax.experimental.pallas import tpu as pltpu

F32 = jnp.float32
BF16 = jnp.bfloat16

GRID_W = 64
RW_HEADS = 8
RW_HEAD_DIM = 64
RW_WIDTH = RW_HEADS * RW_HEAD_DIM
RW_DECAY_LORA = 64
RW_AAA_LORA = 64
RW_GATE_LORA = 128
MLA_HEADS = 4
MLA_Q_LORA = 256
MLA_KV_LORA = 256
MLA_NOPE_DIM = 128
MLA_ROPE_DIM = 64
MLA_V_DIM = 128
MLA_QK_DIM = MLA_NOPE_DIM + MLA_ROPE_DIM
ROPE_BASE = 10000.0
N_EXPERTS = 16
CAPACITY_FACTOR = 2
LN_EPS = 1e-5
RMS_EPS = 1e-6
GN_EPS = 64e-5
RW_PROJ = 3 * RW_WIDTH + 2 * RW_DECAY_LORA + 2 * RW_AAA_LORA + RW_GATE_LORA
MLA_PROJ_EXT = MLA_Q_LORA + MLA_KV_LORA + 2 * MLA_ROPE_DIM
IN_PROJ_EXT = RW_PROJ + MLA_PROJ_EXT

ROW_BLK = 256
CHUNK = 64
LANES = 128
HALO = 8
ATT_TQ = 512
MOE_TM = 512
VMEM_LIMIT = 48 * 1024 * 1024


def _cparams(sem):
    return pltpu.CompilerParams(dimension_semantics=sem, vmem_limit_bytes=VMEM_LIMIT)


def _split2(x):
    hi = x.astype(BF16)
    lo = (x - hi.astype(F32)).astype(BF16)
    return hi, lo


def _split3(x):
    hi = x.astype(BF16)
    r = x - hi.astype(F32)
    mid = r.astype(BF16)
    lo = (r - mid.astype(F32)).astype(BF16)
    return hi, mid, lo


def _mm3(a, b, trans_a=False, trans_b=False):
    ah, al = _split2(a)
    bh, bl = _split2(b)
    d = functools.partial(pl.dot, trans_a=trans_a, trans_b=trans_b)
    return d(ah, bh) + (d(ah, bl) + d(al, bh))


def _mm_sel_rhs(a, sel):
    ah, al = _split2(a)
    return pl.dot(ah, sel) + pl.dot(al, sel)


def _mm_sel_lhs3(sel, b):
    bh, bm, bl = _split3(b)
    return pl.dot(sel, bh) + (pl.dot(sel, bm) + pl.dot(sel, bl))


def _sigmoid(x):
    return 1.0 / (1.0 + jnp.exp(-x))


def _standardize(x, eps):
    mu = jnp.mean(x, axis=-1, keepdims=True)
    xc = x - mu
    var = jnp.mean(xc * xc, axis=-1, keepdims=True)
    return xc * lax.rsqrt(var + eps)


def _ada_kernel(c_ref, w_ref, b_ref, o_ref):
    c = c_ref[...]
    s = c * _sigmoid(c)
    o_ref[...] = _mm3(s, w_ref[...]) + b_ref[...]


def _ada(cc, w, b):
    rows, d = cc.shape
    n = w.shape[1]
    return pl.pallas_call(
        _ada_kernel,
        out_shape=jax.ShapeDtypeStruct((rows, n), F32),
        grid=(n // d,),
        in_specs=[pl.BlockSpec((rows, d), lambda j: (0, 0)),
                  pl.BlockSpec((d, d), lambda j: (0, j)),
                  pl.BlockSpec((1, d), lambda j: (0, j))],
        out_specs=pl.BlockSpec((rows, d), lambda j: (0, j)),
        compiler_params=_cparams(("arbitrary",)),
        name="ada",
    )(cc, w, b)


def _inproj_kernel(x_ref, ss_ref, w_ref, o_ref):
    xn = _standardize(x_ref[0], LN_EPS)
    shift = ss_ref[0, 0, 0:1, :]
    scale = ss_ref[0, 0, 1:2, :]
    m = (xn * (1.0 + scale) + shift).astype(BF16)
    o_ref[0] = pl.dot(m, w_ref[...])


def _inproj(xcat, ss, w_ext):
    b, tt, d = xcat.shape
    n = w_ext.shape[1]
    return pl.pallas_call(
        _inproj_kernel,
        out_shape=jax.ShapeDtypeStruct((b, tt, n), F32),
        grid=(b, tt // ROW_BLK),
        in_specs=[pl.BlockSpec((1, ROW_BLK, d), lambda bi, i: (bi, i, 0)),
                  pl.BlockSpec((1, 1, 2, d), lambda bi, i: (bi, jnp.minimum(i, 1), 0, 0)),
                  pl.BlockSpec((d, n), lambda bi, i: (0, 0))],
        out_specs=pl.BlockSpec((1, ROW_BLK, n), lambda bi, i: (bi, i, 0)),
        compiler_params=_cparams(("arbitrary", "arbitrary")),
        name="inproj",
    )(xcat, ss, w_ext)


def _rwkv_block(d, i, nblk):
    return jnp.where(d == 0, i, jnp.where(i == 0, 0, nblk - i))


def _rwkv_kernel(hm_ref, hp_ref, hn_ref, conv_ref, w0_ref, wup_ref, a0_ref, aup_ref, gup_ref, vec_ref, bd_ref,
                 y_ref, bg_ref, g_ref,
                 s_ref, rt_ref, kt_ref, at_ref, bt_ref, v_ref, a2_ref, k2_ref, gc_ref):
    d = pl.program_id(0)
    i = pl.program_id(2)
    nblk = pl.num_programs(2)
    pb = _rwkv_block(d, i, nblk)
    fwd = d == 0
    w3 = 3 * RW_WIDTH

    @pl.when(i == 0)
    def _():
        s_ref[...] = jnp.zeros_like(s_ref)

    rkv = hm_ref[0, :, 0:w3]
    first = jnp.logical_or(pb == 0, pb == 1)
    last = jnp.logical_or(pb == 0, pb == nblk - 1)
    prev_row = jnp.where(first, 0.0, hp_ref[0, HALO - 1:HALO, :])
    next_row = jnp.where(last, 0.0, hn_ref[0, 0:1, :])
    rows = lax.broadcasted_iota(jnp.int32, (ROW_BLK, 1), 0)
    up = jnp.where(rows == 0, prev_row, pltpu.roll(rkv, 1, 0))
    dn = jnp.where(rows == ROW_BLK - 1, next_row, pltpu.roll(rkv, ROW_BLK - 1, 0))
    rkv = conv_ref[0:1, :] * up + conv_ref[1:2, :] * rkv + conv_ref[2:3, :] * dn
    r = rkv[:, 0:RW_WIDTH]
    k = rkv[:, RW_WIDTH:2 * RW_WIDTH]
    v = rkv[:, 2 * RW_WIDTH:w3]

    wdn2 = hm_ref[0, :, w3:w3 + 2 * RW_DECAY_LORA]
    adn2 = hm_ref[0, :, w3 + 2 * RW_DECAY_LORA:w3 + 2 * RW_DECAY_LORA + 2 * RW_AAA_LORA]
    gdn = hm_ref[0, :, RW_PROJ - RW_GATE_LORA:RW_PROJ]
    wdn = jnp.where(fwd, wdn2[:, :RW_DECAY_LORA], wdn2[:, RW_DECAY_LORA:])
    adn = jnp.where(fwd, adn2[:, :RW_AAA_LORA], adn2[:, RW_AAA_LORA:])

    k_k = vec_ref[0:1, :]
    k_a = vec_ref[1:2, :]
    r_k = vec_ref[2:3, :]
    bd = bd_ref[...]

    z = w0_ref[0] + _mm3(jnp.tanh(wdn), wup_ref[0])
    ld = (-float(np.exp(-0.5))) * _sigmoid(z)
    a = _sigmoid(a0_ref[0] + _mm3(adn, aup_ref[0]))
    g = _mm3(_sigmoid(gdn), gup_ref[...])
    kd = k * (1.0 + (a - 1.0) * k_a)
    kkr = k * k_k
    kk = kkr * lax.rsqrt(_mm_sel_rhs(kkr * kkr, bd) + 1e-12)
    g_ref[0, 0] = g
    bg_ref[0, 0] = _mm_sel_rhs(r * kd * r_k, bd) * v * g

    ri = lax.broadcasted_iota(jnp.int32, (ROW_BLK, ROW_BLK), 0)
    ci = lax.broadcasted_iota(jnp.int32, (ROW_BLK, ROW_BLK), 1)
    same = (ri & -CHUNK) == (ci & -CHUNK)
    sgn = jnp.where(fwd, 1, -1)
    tri = jnp.logical_and(same, (ri - ci) * sgn >= 0)
    cl = _mm_sel_lhs3(jnp.where(tri, 1.0, 0.0).astype(BF16), ld)
    tot = _mm_sel_lhs3(jnp.where(same, 1.0, 0.0).astype(BF16), ld)
    e_neg = jnp.exp(-cl)
    e_rem = jnp.exp(tot - cl)
    ka = kk * a
    rt_ref[...] = r * jnp.exp(cl)
    kt_ref[...] = kd * e_neg
    at_ref[...] = -ka * e_neg
    bt_ref[...] = kk * jnp.exp(cl - ld)
    v_ref[...] = v
    a2_ref[...] = -ka * e_rem
    k2_ref[...] = kd * e_rem
    gc_ref[...] = jnp.exp(tot)

    ii = lax.broadcasted_iota(jnp.int32, (CHUNK, 2 * CHUNK), 0)
    jj = lax.broadcasted_iota(jnp.int32, (CHUNK, 2 * CHUNK), 1) & (CHUNK - 1)
    strict = (ii - jj) * sgn > 0
    incl = (ii - jj) * sgn >= 0
    zeros = jnp.zeros((CHUNK, RW_HEAD_DIM), F32)

    def chunk_body(c, carry):
        cc = jnp.where(fwd, c, ROW_BLK // CHUNK - 1 - c)
        rs = pl.ds(pl.multiple_of(cc * CHUNK, CHUNK), CHUNK)
        for h in range(RW_HEADS):
            hs = slice(h * RW_HEAD_DIM, (h + 1) * RW_HEAD_DIM)
            rt = rt_ref[rs, hs]
            bt = bt_ref[rs, hs]
            vv = v_ref[rs, hs]
            ak = jnp.concatenate([at_ref[rs, hs], kt_ref[rs, hs]], axis=0)
            ak2 = jnp.concatenate([a2_ref[rs, hs], k2_ref[rs, hs]], axis=0)
            gc = gc_ref[pl.ds(pl.multiple_of(cc * CHUNK, CHUNK), 1), hs]
            s0 = s_ref[h]
            ll = jnp.where(strict, _mm3(bt, ak, trans_b=True), 0.0)
            mm = jnp.where(incl, _mm3(rt, ak, trans_b=True), 0.0)
            lbkv = _mm3(ll, jnp.concatenate([zeros, vv], axis=0))
            x = jnp.concatenate([bt, lbkv], axis=1)
            p = ll[:, :CHUNK]
            pows = [p]
            for _ in range(5):
                p = _mm3(p, p)
                pows.append(p)
            for p in reversed(pows):
                x = x + _mm3(p, x)
            u = x[:, RW_HEAD_DIM:] + _mm3(x[:, :RW_HEAD_DIM], s0, trans_b=True)
            uv = jnp.concatenate([u, vv], axis=0)
            y = _mm3(rt, s0, trans_b=True) + _mm3(mm, uv)
            y_ref[0, 0, rs, hs] = y
            s_ref[h] = s0 * gc + _mm3(uv, ak2, trans_a=True)
        return carry

    lax.fori_loop(0, ROW_BLK // CHUNK, chunk_body, 0)


def _rwkv(hcat, conv, w0, wup, a0, aup, gup, vecs, bd):
    b, tt, _ = hcat.shape
    nblk = tt // ROW_BLK
    t = tt - ROW_BLK
    w3 = 3 * RW_WIDTH
    hb = ROW_BLK // HALO

    def pbf(d, i):
        return _rwkv_block(d, i, nblk)

    def obf(d, i):
        return _rwkv_block(d, jnp.maximum(i, 1), nblk) - 1

    out_spec = pl.BlockSpec((1, 1, ROW_BLK, RW_WIDTH), lambda d, bi, i: (d, bi, obf(d, i), 0))
    full = lambda shape: pl.BlockSpec(shape, lambda d, bi, i: (0,) * len(shape))
    perdir = lambda shape: pl.BlockSpec((1,) + shape, lambda d, bi, i: (d,) + (0,) * len(shape))
    scr = lambda: pltpu.VMEM((ROW_BLK, RW_WIDTH), F32)
    return pl.pallas_call(
        _rwkv_kernel,
        out_shape=[jax.ShapeDtypeStruct((2, b, t, RW_WIDTH), F32)] * 3,
        grid=(2, b, nblk),
        in_specs=[
            pl.BlockSpec((1, ROW_BLK, RW_PROJ), lambda d, bi, i: (bi, pbf(d, i), 0)),
            pl.BlockSpec((1, HALO, w3), lambda d, bi, i: (bi, jnp.maximum(pbf(d, i) * hb - 1, 0), 0)),
            pl.BlockSpec((1, HALO, w3), lambda d, bi, i: (bi, jnp.minimum((pbf(d, i) + 1) * hb, nblk * hb - 1), 0)),
            full((3, w3)),
            perdir((1, RW_WIDTH)), perdir((RW_DECAY_LORA, RW_WIDTH)),
            perdir((1, RW_WIDTH)), perdir((RW_AAA_LORA, RW_WIDTH)),
            full((RW_GATE_LORA, RW_WIDTH)), full((3, RW_WIDTH)), full((RW_WIDTH, RW_WIDTH)),
        ],
        out_specs=[out_spec, out_spec, out_spec],
        scratch_shapes=[pltpu.VMEM((RW_HEADS, RW_HEAD_DIM, RW_HEAD_DIM), F32)] + [scr() for _ in range(8)],
        compiler_params=_cparams(("arbitrary", "arbitrary", "arbitrary")),
        name="rwkv",
    )(hcat, hcat, hcat, conv, w0, wup, a0, aup, gup, vecs, bd)


def _mla_kernel(h_ref, tq_ref, tk_ref, qn_ref, kvn_ref, wq_ref, wkv_ref, q_ref, k_ref, v_ref):
    hq = h_ref[0, :, 0:MLA_Q_LORA]
    hkv = h_ref[0, :, MLA_Q_LORA:MLA_Q_LORA + MLA_KV_LORA]
    kr = h_ref[0, :, MLA_Q_LORA + MLA_KV_LORA:MLA_Q_LORA + MLA_KV_LORA + MLA_ROPE_DIM]
    krs = h_ref[0, :, MLA_Q_LORA + MLA_KV_LORA + MLA_ROPE_DIM:MLA_PROJ_EXT]

    def rms(x, gain):
        return x * lax.rsqrt(jnp.mean(x * x, axis=-1, keepdims=True) + RMS_EPS) * gain

    qall = pl.dot(rms(hq, qn_ref[...]).astype(BF16), wq_ref[...])
    kvall = pl.dot(rms(hkv, kvn_ref[...]).astype(BF16), wkv_ref[...])
    nn = MLA_HEADS * MLA_NOPE_DIM
    nr = MLA_HEADS * MLA_ROPE_DIM
    qrot = qall[:, nn:nn + nr] * tq_ref[:, 0:nr] + qall[:, nn + nr:nn + 2 * nr] * tq_ref[:, nr:2 * nr]
    krot = kr * tk_ref[:, 0:MLA_ROPE_DIM] + krs * tk_ref[:, MLA_ROPE_DIM:2 * MLA_ROPE_DIM]
    scale = MLA_QK_DIM ** -0.5
    for h in range(MLA_HEADS):
        qh = jnp.concatenate([qall[:, h * MLA_NOPE_DIM:(h + 1) * MLA_NOPE_DIM],
                              qrot[:, h * MLA_ROPE_DIM:(h + 1) * MLA_ROPE_DIM]], axis=1)
        q_ref[0, h] = (qh * scale).astype(BF16)
        kh = jnp.concatenate([kvall[:, h * MLA_NOPE_DIM:(h + 1) * MLA_NOPE_DIM], krot], axis=1)
        k_ref[0, h] = kh.astype(BF16)
        v_ref[0, h] = kvall[:, nn + h * MLA_V_DIM:nn + (h + 1) * MLA_V_DIM].astype(BF16)


def _mla_qkv(hcat, tabq, tabk, qn, kvn, wq, wkv):
    b, tt, _ = hcat.shape
    t = tt - ROW_BLK
    lat = lambda i: jnp.maximum(i, 1) - 1
    return pl.pallas_call(
        _mla_kernel,
        out_shape=[jax.ShapeDtypeStruct((b, MLA_HEADS, t, MLA_QK_DIM), BF16),
                   jax.ShapeDtypeStruct((b, MLA_HEADS, tt, MLA_QK_DIM), BF16),
                   jax.ShapeDtypeStruct((b, MLA_HEADS, tt, MLA_V_DIM), BF16)],
        grid=(b, tt // ROW_BLK),
        in_specs=[pl.BlockSpec((1, ROW_BLK, MLA_PROJ_EXT), lambda bi, i: (bi, i, RW_PROJ // MLA_PROJ_EXT)),
                  pl.BlockSpec((ROW_BLK, 2 * MLA_HEADS * MLA_ROPE_DIM), lambda bi, i: (lat(i), 0)),
                  pl.BlockSpec((ROW_BLK, 2 * MLA_ROPE_DIM), lambda bi, i: (i, 0)),
                  pl.BlockSpec((1, MLA_Q_LORA), lambda bi, i: (0, 0)),
                  pl.BlockSpec((1, MLA_KV_LORA), lambda bi, i: (0, 0)),
                  pl.BlockSpec(wq.shape, lambda bi, i: (0, 0)),
                  pl.BlockSpec(wkv.shape, lambda bi, i: (0, 0))],
        out_specs=[pl.BlockSpec((1, MLA_HEADS, ROW_BLK, MLA_QK_DIM), lambda bi, i: (bi, 0, lat(i), 0)),
                   pl.BlockSpec((1, MLA_HEADS, ROW_BLK, MLA_QK_DIM), lambda bi, i: (bi, 0, i, 0)),
                   pl.BlockSpec((1, MLA_HEADS, ROW_BLK, MLA_V_DIM), lambda bi, i: (bi, 0, i, 0))],
        compiler_params=_cparams(("arbitrary", "arbitrary")),
        name="mla_qkv",
    )(hcat, tabq, tabk, qn, kvn, wq, wkv)


def _attn_kernel(q_ref, k_ref, v_ref, o_ref, m_ref, l_ref, acc_ref):
    ki = pl.program_id(3)

    @pl.when(ki == 0)
    def _():
        m_ref[...] = jnp.full_like(m_ref, -jnp.inf)
        l_ref[...] = jnp.zeros_like(l_ref)
        acc_ref[...] = jnp.zeros_like(acc_ref)

    s = pl.dot(q_ref[0, 0], k_ref[0, 0], trans_b=True)
    m_prev = m_ref[...]
    m_new = jnp.maximum(m_prev, jnp.max(s, axis=-1, keepdims=True))
    p = jnp.exp(s - m_new)
    alpha = jnp.exp(m_prev - m_new)
    l_ref[...] = alpha * l_ref[...] + jnp.sum(p, axis=-1, keepdims=True)
    acc_ref[...] = alpha * acc_ref[...] + pl.dot(p.astype(BF16), v_ref[0, 0])
    m_ref[...] = m_new

    @pl.when(ki == pl.num_programs(3) - 1)
    def _():
        o_ref[0] = acc_ref[...] / l_ref[...]


def _attn_tk(tk_total):
    for cand in (768, 1024, 512, 384, 256):
        if tk_total % cand == 0:
            return cand
    return ROW_BLK


def _attention(q, k, v):
    b, nh, t, dq = q.shape
    tk_total = k.shape[2]
    tq = min(ATT_TQ, t)
    tk = _attn_tk(tk_total)
    return pl.pallas_call(
        _attn_kernel,
        out_shape=jax.ShapeDtypeStruct((b, t, nh * MLA_V_DIM), F32),
        grid=(b, nh, t // tq, tk_total // tk),
        in_specs=[pl.BlockSpec((1, 1, tq, dq), lambda bi, h, qi, ki: (bi, h, qi, 0)),
                  pl.BlockSpec((1, 1, tk, dq), lambda bi, h, qi, ki: (bi, h, ki, 0)),
                  pl.BlockSpec((1, 1, tk, MLA_V_DIM), lambda bi, h, qi, ki: (bi, h, ki, 0))],
        out_specs=pl.BlockSpec((1, tq, MLA_V_DIM), lambda bi, h, qi, ki: (bi, qi, h)),
        scratch_shapes=[pltpu.VMEM((tq, 1), F32), pltpu.VMEM((tq, 1), F32), pltpu.VMEM((tq, MLA_V_DIM), F32)],
        compiler_params=_cparams(("arbitrary", "arbitrary", "arbitrary", "arbitrary")),
        name="attn",
    )(q, k, v)


def _mix_kernel(alpha, y_ref, bg_ref, g_ref, o_ref, x_ref, wout_ref, gn_ref, ln_ref, mod_ref, rt_ref, bd_ref,
                x1_ref, u_ref, aff_ref):
    bd = bd_ref[...]
    inv_n = 1.0 / RW_HEAD_DIM
    ysum = y_ref[0, 0] + y_ref[1, 0]
    yc = ysum - _mm_sel_rhs(ysum, bd) * inv_n
    var = _mm_sel_rhs(yc * yc, bd) * inv_n
    yn = yc * lax.rsqrt(var + GN_EPS) * gn_ref[0:1, :] + gn_ref[1:2, :]
    rw = yn * g_ref[0, 0] + (bg_ref[0, 0] + bg_ref[1, 0])
    mix = (pl.dot(rw.astype(BF16), wout_ref[0:RW_WIDTH, :])
           + pl.dot(o_ref[0].astype(BF16), wout_ref[RW_WIDTH:, :]))
    g1 = mod_ref[0, 0:1, :]
    sh2 = mod_ref[0, 1:2, :]
    sc2 = mod_ref[0, 2:3, :]
    x1 = _standardize(alpha * x_ref[0] + g1 * mix, LN_EPS) * ln_ref[0:1, :] + ln_ref[1:2, :]
    x1_ref[0] = x1
    u = _standardize(x1, LN_EPS) * (1.0 + sc2) + sh2
    u_ref[0] = u.astype(BF16)
    logits = _mm3(rt_ref[...], u, trans_b=True)
    ex = jnp.exp(logits - jnp.max(logits, axis=0, keepdims=True))
    aff_ref[0] = ex / jnp.sum(ex, axis=0, keepdims=True)


def _mix(alpha, y2, bg2, g2, o_mla, x, wout, gn, ln, modv, router_t, bd):
    b, t, d = x.shape
    ne = router_t.shape[0]
    row = lambda w: pl.BlockSpec((1, ROW_BLK, w), lambda bi, i: (bi, i, 0))
    two = pl.BlockSpec((2, 1, ROW_BLK, RW_WIDTH), lambda bi, i: (0, bi, i, 0))
    full = lambda a: pl.BlockSpec(a.shape, lambda bi, i: (0,) * a.ndim)
    return pl.pallas_call(
        functools.partial(_mix_kernel, alpha),
        out_shape=[jax.ShapeDtypeStruct((b, t, d), F32),
                   jax.ShapeDtypeStruct((b, t, d), BF16),
                   jax.ShapeDtypeStruct((b, ne, t), F32)],
        grid=(b, t // ROW_BLK),
        in_specs=[two, two, pl.BlockSpec((1, 1, ROW_BLK, RW_WIDTH), lambda bi, i: (0, bi, i, 0)),
                  row(MLA_HEADS * MLA_V_DIM), row(d), full(wout), full(gn), full(ln),
                  pl.BlockSpec((1, 3, d), lambda bi, i: (bi, 0, 0)), full(router_t), full(bd)],
        out_specs=[row(d), row(d), pl.BlockSpec((1, ne, ROW_BLK), lambda bi, i: (bi, 0, i))],
        compiler_params=_cparams(("arbitrary", "arbitrary")),
        name="mix",
    )(y2, bg2, g2, o_mla, x, wout, gn, ln, modv, router_t, bd)


def _topk_kernel(cap, aff_ref, w_ref):
    aff = aff_ref[0]
    ne, t = aff.shape
    bits = lax.bitcast_convert_type(aff, jnp.int32)

    def count_ge(thr):
        return jnp.sum(jnp.where(bits >= thr, 1.0, 0.0), axis=1, keepdims=True)

    def body(it, thr):
        cand = thr | lax.shift_left(jnp.int32(1), 30 - it)
        return jnp.where(count_ge(cand) >= cap, cand, thr)

    thr = lax.fori_loop(0, 31, body, jnp.zeros((ne, 1), jnp.int32))
    need = cap - jnp.sum(jnp.where(bits > thr, 1.0, 0.0), axis=1, keepdims=True)
    ri = lax.broadcasted_iota(jnp.int32, (LANES, LANES), 0)
    ci = lax.broadcasted_iota(jnp.int32, (LANES, LANES), 1)
    below = jnp.where(ri < ci, 1.0, 0.0).astype(BF16)
    carry = jnp.zeros((ne, 1), F32)
    for j in range(t // LANES):
        sl = slice(j * LANES, (j + 1) * LANES)
        bj = bits[:, sl]
        eqj = bj == thr
        eqf = jnp.where(eqj, 1.0, 0.0)
        before = pl.dot(eqf.astype(BF16), below) + carry
        take = jnp.logical_or(bj > thr, jnp.logical_and(eqj, before < need))
        w_ref[0, :, sl] = jnp.where(take, aff[:, sl], 0.0)
        carry = carry + jnp.sum(eqf, axis=1, keepdims=True)


def _topk(aff, cap):
    b, ne, t = aff.shape
    return pl.pallas_call(
        functools.partial(_topk_kernel, float(cap)),
        out_shape=jax.ShapeDtypeStruct((b, ne, t), F32),
        grid=(b,),
        in_specs=[pl.BlockSpec((1, ne, t), lambda bi: (bi, 0, 0))],
        out_specs=pl.BlockSpec((1, ne, t), lambda bi: (bi, 0, 0)),
        compiler_params=_cparams(("arbitrary",)),
        name="topk",
    )(aff)


def _moe_kernel(alpha, u_ref, wt_ref, x1_ref, wg_ref, wu_ref, wd_ref, g2_ref, ln_ref, o_ref, acc_ref):
    e = pl.program_id(2)

    @pl.when(e == 0)
    def _():
        acc_ref[...] = jnp.zeros_like(acc_ref)

    u = u_ref[0]
    hg = pl.dot(u, wg_ref[0])
    hu = pl.dot(u, wu_ref[0])
    hid = (hg * _sigmoid(hg) * hu).astype(BF16)
    ye = pl.dot(hid, wd_ref[0])
    wt = wt_ref[0]
    lane = lax.broadcasted_iota(jnp.int32, wt.shape, 1)
    wcol = jnp.sum(jnp.where(lane == e, wt, 0.0), axis=1, keepdims=True)
    acc_ref[...] += wcol * ye

    @pl.when(e == pl.num_programs(2) - 1)
    def _():
        xr = alpha * x1_ref[0] + g2_ref[0] * acc_ref[...]
        o_ref[0] = _standardize(xr, LN_EPS) * ln_ref[0:1, :] + ln_ref[1:2, :]


def _moe(alpha, u, wt, x1, wg, wu, wd, g2, ln):
    b, t, d = x1.shape
    ne, _, f = wg.shape
    tm = min(MOE_TM, t)
    row = lambda w: pl.BlockSpec((1, tm, w), lambda bi, i, e: (bi, i, 0))
    return pl.pallas_call(
        functools.partial(_moe_kernel, alpha),
        out_shape=jax.ShapeDtypeStruct((b, t, d), F32),
        grid=(b, t // tm, ne),
        in_specs=[row(d), row(ne), row(d),
                  pl.BlockSpec((1, d, f), lambda bi, i, e: (e, 0, 0)),
                  pl.BlockSpec((1, d, f), lambda bi, i, e: (e, 0, 0)),
                  pl.BlockSpec((1, f, d), lambda bi, i, e: (e, 0, 0)),
                  pl.BlockSpec((1, 1, d), lambda bi, i, e: (bi, 0, 0)),
                  pl.BlockSpec((2, d), lambda bi, i, e: (0, 0))],
        out_specs=row(d),
        scratch_shapes=[pltpu.VMEM((tm, d), F32)],
        compiler_params=_cparams(("arbitrary", "arbitrary", "arbitrary")),
        name="moe",
    )(u, wt, x1, wg, wu, wd, g2, ln)


def _rope_tables(t, tc):
    half = MLA_ROPE_DIM // 2
    inv_freq = ROPE_BASE ** (-jnp.arange(0, half, 2, dtype=F32) / half)
    pos = jnp.arange(t)
    ang_r = (pos // GRID_W).reshape(-1, 1).astype(F32) * inv_freq
    ang_c = (pos % GRID_W).reshape(-1, 1).astype(F32) * inv_freq
    cos = jnp.concatenate([jnp.cos(ang_r)] * 2 + [jnp.cos(ang_c)] * 2, axis=-1)
    sin = jnp.concatenate([-jnp.sin(ang_r), jnp.sin(ang_r), -jnp.sin(ang_c), jnp.sin(ang_c)], axis=-1)
    tabq = jnp.concatenate([jnp.tile(cos, (1, MLA_HEADS)), jnp.tile(sin, (1, MLA_HEADS))], axis=-1)
    cos_k = jnp.concatenate([jnp.ones((tc, MLA_ROPE_DIM), F32), cos], axis=0)
    sin_k = jnp.concatenate([jnp.zeros((tc, MLA_ROPE_DIM), F32), sin], axis=0)
    return tabq, jnp.concatenate([cos_k, sin_k], axis=-1)


def _pair_swap():
    q = MLA_ROPE_DIM // 4
    return np.concatenate([np.arange(q, 2 * q), np.arange(0, q), np.arange(3 * q, 4 * q), np.arange(2 * q, 3 * q)])


def kernel(x, c, ctx, c_ctx, w_ada, b_ada, w_in, rwkv_conv, rwkv_w0, rwkv_w_up, rwkv_a0, rwkv_a_up, rwkv_g_up, rwkv_k_k, rwkv_k_a, rwkv_r_k, rwkv_gn_g, rwkv_gn_b, mla_q_norm, mla_w_uq, mla_kv_norm, mla_w_uk, mla_w_uv, w_out, ln1_g, ln1_b, router, exp_w_gate, exp_w_up, exp_w_down, ln2_g, ln2_b):
    b, t, d = x.shape
    tc = ctx.shape[1]
    depth = w_ada.shape[0]
    assert depth == 1 and tc == ROW_BLK and t % ROW_BLK == 0 and t % LANES == 0
    alpha = (2.0 * depth) ** 0.25
    cap = CAPACITY_FACTOR * t // N_EXPERTS

    pad = (-(b + 1)) % 8
    cc = jnp.concatenate([c, c_ctx[None, :], jnp.zeros((pad, d), F32)], axis=0)
    mod = _ada(cc, w_ada[0], b_ada)
    sh1, sc1, g1, sh2, sc2, g2 = jnp.split(mod, 6, axis=-1)
    ss_lat = jnp.stack([sh1[:b], sc1[:b]], axis=1)
    ss_ctx = jnp.broadcast_to(jnp.stack([sh1[b], sc1[b]], axis=0)[None], (b, 2, d))
    ss = jnp.stack([ss_ctx, ss_lat], axis=1)

    swap = _pair_swap()
    kr0 = RW_PROJ + MLA_Q_LORA + MLA_KV_LORA
    w_ext = jnp.concatenate([w_in[0], w_in[0][:, kr0 + swap]], axis=1).astype(BF16)
    xcat = jnp.concatenate([ctx, x], axis=1)
    hcat = _inproj(xcat, ss, w_ext)

    head_ones = np.kron(np.eye(RW_HEADS, dtype=np.float32), np.ones((RW_HEAD_DIM, RW_HEAD_DIM), np.float32))
    bd = jnp.asarray(head_ones, BF16)
    vecs = jnp.stack([rwkv_k_k[0], rwkv_k_a[0], rwkv_r_k[0].reshape(-1)], axis=0)
    y2, bg2, gg2 = _rwkv(hcat, rwkv_conv[0], rwkv_w0[0][:, None, :], rwkv_w_up[0], rwkv_a0[0][:, None, :],
                         rwkv_a_up[0], rwkv_g_up[0], vecs, bd)

    wq = mla_w_uq[0].reshape(MLA_Q_LORA, MLA_HEADS, MLA_QK_DIM)
    wq_nope = wq[:, :, :MLA_NOPE_DIM].reshape(MLA_Q_LORA, -1)
    wq_rope = wq[:, :, MLA_NOPE_DIM:]
    wq_ext = jnp.concatenate([wq_nope, wq_rope.reshape(MLA_Q_LORA, -1),
                              wq_rope[:, :, swap].reshape(MLA_Q_LORA, -1)], axis=1).astype(BF16)
    wkv = jnp.concatenate([mla_w_uk[0], mla_w_uv[0]], axis=1).astype(BF16)
    tabq, tabk = _rope_tables(t, tc)
    q, k, v = _mla_qkv(hcat, tabq, tabk, mla_q_norm, mla_kv_norm, wq_ext, wkv)
    o_mla = _attention(q, k, v)

    gn = jnp.stack([rwkv_gn_g[0], rwkv_gn_b[0]], axis=0)
    ln1 = jnp.stack([ln1_g[0], ln1_b[0]], axis=0)
    modv = jnp.stack([g1[:b], sh2[:b], sc2[:b]], axis=1)
    x1, u, aff = _mix(alpha, y2, bg2, gg2, o_mla, x, w_out[0].astype(BF16), gn, ln1, modv, router[0].T, bd)

    wt = jnp.swapaxes(_topk(aff, cap), 1, 2)
    ln2 = jnp.stack([ln2_g[0], ln2_b[0]], axis=0)
    return _moe(alpha, u, wt, x1, exp_w_gate[0].astype(BF16), exp_w_up[0].astype(BF16),
                exp_w_down[0].astype(BF16), g2[:b, None, :], ln2)
```

```python
import functools

import jax
import jax.numpy as jnp
import numpy as np
from jax import lax
from jax.experimental import pallas as pl
from jax.experimental.pallas import tpu as pltpu

F32 = jnp.float32
BF16 = jnp.bfloat16

GRID_W = 64
RW_HEADS = 8
RW_HEAD_DIM = 64
RW_WIDTH = RW_HEADS * RW_HEAD_DIM
RW_DECAY_LORA = 64
RW_AAA_LORA = 64
RW_GATE_LORA = 128
MLA_HEADS = 4
MLA_Q_LORA = 256
MLA_KV_LORA = 256
MLA_NOPE_DIM = 128
MLA_ROPE_DIM = 64
MLA_V_DIM = 128
MLA_QK_DIM = MLA_NOPE_DIM + MLA_ROPE_DIM
MLA_V_EXT = 256
ROPE_BASE = 10000.0
N_EXPERTS = 16
CAPACITY_FACTOR = 2
LN_EPS = 1e-5
RMS_EPS = 1e-6
GN_EPS = 64e-5
RW_PROJ = 3 * RW_WIDTH + 2 * RW_DECAY_LORA + 2 * RW_AAA_LORA + RW_GATE_LORA
MLA_PROJ_EXT = MLA_Q_LORA + MLA_KV_LORA + 2 * MLA_ROPE_DIM
IN_PROJ_EXT = RW_PROJ + MLA_PROJ_EXT

ROW_BLK = 256
CHUNK = 64
LANES = 128
HALO = 8
GROUP_LANES = 256
GROUP_HEADS = GROUP_LANES // RW_HEAD_DIM
ATT_TQ = 512
MOE_TM = 512
VMEM_LIMIT = 48 * 1024 * 1024


def _cparams(sem):
    return pltpu.CompilerParams(dimension_semantics=sem, vmem_limit_bytes=VMEM_LIMIT)


def _split2(x):
    hi = x.astype(BF16)
    lo = (x - hi.astype(F32)).astype(BF16)
    return hi, lo


def _split3(x):
    hi = x.astype(BF16)
    r = x - hi.astype(F32)
    mid = r.astype(BF16)
    lo = (r - mid.astype(F32)).astype(BF16)
    return hi, mid, lo


def _mm3(a, b, trans_a=False, trans_b=False):
    ah, al = _split2(a)
    bh, bl = _split2(b)
    d = functools.partial(pl.dot, trans_a=trans_a, trans_b=trans_b)
    return d(ah, bh) + (d(ah, bl) + d(al, bh))


def _mm_sel_rhs(a, sel):
    ah, al = _split2(a)
    return pl.dot(ah, sel) + pl.dot(al, sel)


def _mm_sel_lhs3(sel, b):
    bh, bm, bl = _split3(b)
    return pl.dot(sel, bh) + (pl.dot(sel, bm) + pl.dot(sel, bl))


def _sigmoid(x):
    return 1.0 / (1.0 + jnp.exp(-x))


def _standardize(x, eps):
    mu = jnp.mean(x, axis=-1, keepdims=True)
    xc = x - mu
    var = jnp.mean(xc * xc, axis=-1, keepdims=True)
    return xc * lax.rsqrt(var + eps)


def _ada_kernel(c_ref, w_ref, b_ref, o_ref):
    c = c_ref[...]
    s = c * _sigmoid(c)
    o_ref[...] = _mm3(s, w_ref[...]) + b_ref[...]


def _ada(cc, w, b):
    rows, d = cc.shape
    n = w.shape[1]
    return pl.pallas_call(
        _ada_kernel,
        out_shape=jax.ShapeDtypeStruct((rows, n), F32),
        grid=(n // d,),
        in_specs=[pl.BlockSpec((rows, d), lambda j: (0, 0)),
                  pl.BlockSpec((d, d), lambda j: (0, j)),
                  pl.BlockSpec((1, d), lambda j: (0, j))],
        out_specs=pl.BlockSpec((rows, d), lambda j: (0, j)),
        compiler_params=_cparams(("arbitrary",)),
        name="ada",
    )(cc, w, b)


def _inproj_kernel(x_ref, ss_ref, w_ref, o_ref):
    xn = _standardize(x_ref[0], LN_EPS)
    shift = ss_ref[0, 0, 0:1, :]
    scale = ss_ref[0, 0, 1:2, :]
    m = (xn * (1.0 + scale) + shift).astype(BF16)
    o_ref[0] = pl.dot(m, w_ref[...])


def _inproj(xcat, ss, w_ext):
    b, tt, d = xcat.shape
    n = w_ext.shape[1]
    return pl.pallas_call(
        _inproj_kernel,
        out_shape=jax.ShapeDtypeStruct((b, tt, n), F32),
        grid=(b, tt // ROW_BLK),
        in_specs=[pl.BlockSpec((1, ROW_BLK, d), lambda bi, i: (bi, i, 0)),
                  pl.BlockSpec((1, 1, 2, d), lambda bi, i: (bi, jnp.minimum(i, 1), 0, 0)),
                  pl.BlockSpec((d, n), lambda bi, i: (0, 0))],
        out_specs=pl.BlockSpec((1, ROW_BLK, n), lambda bi, i: (bi, i, 0)),
        compiler_params=_cparams(("arbitrary", "arbitrary")),
        name="inproj",
    )(xcat, ss, w_ext)


def _rwkv_block(d, i, nblk):
    return jnp.where(d == 0, i, jnp.where(i == 0, 0, nblk - i))


def _rwkv_kernel(hm_ref, hp_ref, hn_ref, conv_ref, w0_ref, wup_ref, a0_ref, aup_ref, gup_ref, vec_ref, bd_ref,
                 y_ref, bg_ref, g_ref,
                 s_ref, rt_ref, kt_ref, at_ref, bt_ref, v_ref, a2_ref, k2_ref, gc_ref, w_ref, ub_ref, mra_ref, yv_ref):
    d = pl.program_id(0)
    i = pl.program_id(2)
    nblk = pl.num_programs(2)
    pb = _rwkv_block(d, i, nblk)
    fwd = d == 0
    w3 = 3 * RW_WIDTH

    @pl.when(i == 0)
    def _():
        s_ref[...] = jnp.zeros_like(s_ref)

    rkv = hm_ref[0, :, 0:w3]
    first = jnp.logical_or(pb == 0, pb == 1)
    last = jnp.logical_or(pb == 0, pb == nblk - 1)
    prev_row = jnp.where(first, 0.0, hp_ref[0, HALO - 1:HALO, :])
    next_row = jnp.where(last, 0.0, hn_ref[0, 0:1, :])
    rows = lax.broadcasted_iota(jnp.int32, (ROW_BLK, 1), 0)
    up = jnp.where(rows == 0, prev_row, pltpu.roll(rkv, 1, 0))
    dn = jnp.where(rows == ROW_BLK - 1, next_row, pltpu.roll(rkv, ROW_BLK - 1, 0))
    rkv = conv_ref[0:1, :] * up + conv_ref[1:2, :] * rkv + conv_ref[2:3, :] * dn
    r = rkv[:, 0:RW_WIDTH]
    k = rkv[:, RW_WIDTH:2 * RW_WIDTH]
    v = rkv[:, 2 * RW_WIDTH:w3]

    wdn2 = hm_ref[0, :, w3:w3 + 2 * RW_DECAY_LORA]
    adn2 = hm_ref[0, :, w3 + 2 * RW_DECAY_LORA:w3 + 2 * RW_DECAY_LORA + 2 * RW_AAA_LORA]
    gdn = hm_ref[0, :, RW_PROJ - RW_GATE_LORA:RW_PROJ]
    wdn = jnp.where(fwd, wdn2[:, :RW_DECAY_LORA], wdn2[:, RW_DECAY_LORA:])
    adn = jnp.where(fwd, adn2[:, :RW_AAA_LORA], adn2[:, RW_AAA_LORA:])

    k_k = vec_ref[0:1, :]
    k_a = vec_ref[1:2, :]
    r_k = vec_ref[2:3, :]
    bd = bd_ref[...]

    z = w0_ref[0] + _mm3(jnp.tanh(wdn), wup_ref[0])
    ld = (-float(np.exp(-0.5))) * _sigmoid(z)
    a = _sigmoid(a0_ref[0] + _mm3(adn, aup_ref[0]))
    g = _mm3(_sigmoid(gdn), gup_ref[...])
    kd = k * (1.0 + (a - 1.0) * k_a)
    kkr = k * k_k
    kk = kkr * lax.rsqrt(_mm_sel_rhs(kkr * kkr, bd) + 1e-12)
    g_ref[0, 0] = g
    bg_ref[0, 0] = _mm_sel_rhs(r * kd * r_k, bd) * v * g

    ri = lax.broadcasted_iota(jnp.int32, (ROW_BLK, ROW_BLK), 0)
    ci = lax.broadcasted_iota(jnp.int32, (ROW_BLK, ROW_BLK), 1)
    same = (ri & -CHUNK) == (ci & -CHUNK)
    sgn = jnp.where(fwd, 1, -1)
    tri = jnp.logical_and(same, (ri - ci) * sgn >= 0)
    cl = _mm_sel_lhs3(jnp.where(tri, 1.0, 0.0).astype(BF16), ld)
    tot = _mm_sel_lhs3(jnp.where(same, 1.0, 0.0).astype(BF16), ld)
    e_neg = jnp.exp(-cl)
    e_rem = jnp.exp(tot - cl)
    ka = kk * a
    rt_ref[...] = r * jnp.exp(cl)
    kt_ref[...] = kd * e_neg
    at_ref[...] = -ka * e_neg
    bt_ref[...] = kk * jnp.exp(cl - ld)
    v_ref[...] = v
    a2_ref[...] = -ka * e_rem
    k2_ref[...] = kd * e_rem
    gc_ref[...] = jnp.exp(tot)

    r4 = lax.broadcasted_iota(jnp.int32, (GROUP_LANES, GROUP_LANES), 0) & -RW_HEAD_DIM
    c4 = lax.broadcasted_iota(jnp.int32, (GROUP_LANES, GROUP_LANES), 1) & -RW_HEAD_DIM
    same_head = r4 == c4
    head_mask = jnp.where(same_head, 1.0, 0.0).astype(BF16)
    ii = lax.broadcasted_iota(jnp.int32, (CHUNK, GROUP_LANES), 0)
    jj = lax.broadcasted_iota(jnp.int32, (CHUNK, GROUP_LANES), 1) & (CHUNK - 1)
    strict = (ii - jj) * sgn > 0
    incl = (ii - jj) * sgn >= 0
    eye = jnp.where(ii == jj, 1.0, 0.0)

    def bdiag(x):
        xb = x.astype(BF16)
        return jnp.concatenate([xb] * GROUP_HEADS, axis=0) * head_mask

    def stack(top, bottom):
        return jnp.concatenate([top, bottom], axis=0).astype(BF16)

    n_chunks = ROW_BLK // CHUNK
    chains = [(c, gi) for c in range(n_chunks) for gi in range(RW_WIDTH // GROUP_LANES)]

    def sl(ref, c, gi):
        return ref[c * CHUNK:(c + 1) * CHUNK, gi * GROUP_LANES:(gi + 1) * GROUP_LANES]

    pw, tt, lbkv = {}, {}, {}
    for key in chains:
        c, gi = key
        rows, lanes = slice(c * CHUNK, (c + 1) * CHUNK), slice(gi * GROUP_LANES, (gi + 1) * GROUP_LANES)
        lhs = stack(sl(bt_ref, *key), sl(rt_ref, *key))
        ga = pl.dot(lhs, bdiag(sl(at_ref, *key)), trans_b=True)
        gk = pl.dot(lhs, bdiag(sl(kt_ref, *key)), trans_b=True)
        mra_ref[rows, lanes] = jnp.where(incl, ga[CHUNK:], 0.0)
        pw[key] = jnp.where(strict, ga[:CHUNK], 0.0)
        tt[key] = eye + pw[key]
        lm = stack(jnp.where(strict, gk[:CHUNK], 0.0), jnp.where(incl, gk[CHUNK:], 0.0))
        lv = pl.dot(lm, bdiag(sl(v_ref, *key)))
        lbkv[key] = lv[:CHUNK]
        yv_ref[rows, lanes] = lv[CHUNK:]
    for key in chains:
        pw[key] = pl.dot(pw[key].astype(BF16), bdiag(pw[key]))
    for _ in range(4):
        for key in chains:
            st = pl.dot(stack(pw[key], tt[key]), bdiag(pw[key]))
            pw[key] = st[:CHUNK]
            tt[key] = tt[key] + st[CHUNK:]
    for key in chains:
        c, gi = key
        rows, lanes = slice(c * CHUNK, (c + 1) * CHUNK), slice(gi * GROUP_LANES, (gi + 1) * GROUP_LANES)
        t_inv = (tt[key] + pl.dot(tt[key].astype(BF16), bdiag(pw[key]))).astype(BF16)
        w_ref[rows, lanes] = pl.dot(t_inv, bdiag(sl(bt_ref, *key)))
        ub_ref[rows, lanes] = pl.dot(t_inv, bdiag(lbkv[key]))

    for c in range(n_chunks):
        cc = jnp.where(fwd, c, n_chunks - 1 - c)
        r0 = pl.multiple_of(cc * CHUNK, CHUNK)
        rs = pl.ds(r0, CHUNK)
        for gi in range(RW_WIDTH // GROUP_LANES):
            ls = slice(gi * GROUP_LANES, (gi + 1) * GROUP_LANES)
            s0 = s_ref[:, ls]
            s_hi, s_lo = _split2(s0)
            lhs = stack(w_ref[rs, ls], rt_ref[rs, ls])
            ws = pl.dot(lhs, bdiag(s_hi), trans_b=True) + pl.dot(lhs, bdiag(s_lo), trans_b=True)
            u = ub_ref[rs, ls] + ws[:CHUNK]
            y_ref[0, 0, rs, ls] = ws[CHUNK:] + yv_ref[rs, ls] + pl.dot(mra_ref[rs, ls].astype(BF16), bdiag(u))
            uv = jnp.concatenate([u, v_ref[rs, ls]], axis=0)
            ak2 = jnp.concatenate([a2_ref[rs, ls], k2_ref[rs, ls]], axis=0)
            full = jnp.where(same_head, pl.dot(uv, ak2, trans_a=True), 0.0)
            inc = full[0:RW_HEAD_DIM]
            for hh in range(1, GROUP_HEADS):
                inc = inc + full[hh * RW_HEAD_DIM:(hh + 1) * RW_HEAD_DIM]
            s_ref[:, ls] = s0 * gc_ref[pl.ds(r0, 1), ls] + inc


def _rwkv(hcat, conv, w0, wup, a0, aup, gup, vecs, bd):
    b, tt, _ = hcat.shape
    nblk = tt // ROW_BLK
    t = tt - ROW_BLK
    w3 = 3 * RW_WIDTH
    hb = ROW_BLK // HALO

    def pbf(d, i):
        return _rwkv_block(d, i, nblk)

    def obf(d, i):
        return _rwkv_block(d, jnp.maximum(i, 1), nblk) - 1

    out_spec = pl.BlockSpec((1, 1, ROW_BLK, RW_WIDTH), lambda d, bi, i: (d, bi, obf(d, i), 0))
    full = lambda shape: pl.BlockSpec(shape, lambda d, bi, i: (0,) * len(shape))
    perdir = lambda shape: pl.BlockSpec((1,) + shape, lambda d, bi, i: (d,) + (0,) * len(shape))
    scr = lambda: pltpu.VMEM((ROW_BLK, RW_WIDTH), F32)
    return pl.pallas_call(
        _rwkv_kernel,
        out_shape=[jax.ShapeDtypeStruct((2, b, t, RW_WIDTH), F32)] * 3,
        grid=(2, b, nblk),
        in_specs=[
            pl.BlockSpec((1, ROW_BLK, RW_PROJ), lambda d, bi, i: (bi, pbf(d, i), 0)),
            pl.BlockSpec((1, HALO, w3), lambda d, bi, i: (bi, jnp.maximum(pbf(d, i) * hb - 1, 0), 0)),
            pl.BlockSpec((1, HALO, w3), lambda d, bi, i: (bi, jnp.minimum((pbf(d, i) + 1) * hb, nblk * hb - 1), 0)),
            full((3, w3)),
            perdir((1, RW_WIDTH)), perdir((RW_DECAY_LORA, RW_WIDTH)),
            perdir((1, RW_WIDTH)), perdir((RW_AAA_LORA, RW_WIDTH)),
            full((RW_GATE_LORA, RW_WIDTH)), full((3, RW_WIDTH)), full((RW_WIDTH, RW_WIDTH)),
        ],
        out_specs=[out_spec, out_spec, out_spec],
        scratch_shapes=[pltpu.VMEM((RW_HEAD_DIM, RW_WIDTH), F32)] + [scr() for _ in range(12)],
        compiler_params=_cparams(("arbitrary", "arbitrary", "arbitrary")),
        name="rwkv",
    )(hcat, hcat, hcat, conv, w0, wup, a0, aup, gup, vecs, bd)


def _mla_kernel(h_ref, tq_ref, tk_ref, qn_ref, kvn_ref, wq_ref, wkv_ref, q_ref, k_ref, v_ref):
    hq = h_ref[0, :, 0:MLA_Q_LORA]
    hkv = h_ref[0, :, MLA_Q_LORA:MLA_Q_LORA + MLA_KV_LORA]
    kr = h_ref[0, :, MLA_Q_LORA + MLA_KV_LORA:MLA_Q_LORA + MLA_KV_LORA + MLA_ROPE_DIM]
    krs = h_ref[0, :, MLA_Q_LORA + MLA_KV_LORA + MLA_ROPE_DIM:MLA_PROJ_EXT]

    def rms(x, gain):
        return x * lax.rsqrt(jnp.mean(x * x, axis=-1, keepdims=True) + RMS_EPS) * gain

    qall = pl.dot(rms(hq, qn_ref[...]).astype(BF16), wq_ref[...])
    kvall = pl.dot(rms(hkv, kvn_ref[...]).astype(BF16), wkv_ref[...])
    nn = MLA_HEADS * MLA_NOPE_DIM
    nr = MLA_HEADS * MLA_ROPE_DIM
    qrot = qall[:, nn:nn + nr] * tq_ref[:, 0:nr] + qall[:, nn + nr:nn + 2 * nr] * tq_ref[:, nr:2 * nr]
    krot = kr * tk_ref[:, 0:MLA_ROPE_DIM] + krs * tk_ref[:, MLA_ROPE_DIM:2 * MLA_ROPE_DIM]
    scale = MLA_QK_DIM ** -0.5
    ones_col = jnp.where(lax.broadcasted_iota(jnp.int32, (ROW_BLK, MLA_V_EXT - MLA_V_DIM), 1) == 0, 1.0, 0.0)
    for h in range(MLA_HEADS):
        qh = jnp.concatenate([qall[:, h * MLA_NOPE_DIM:(h + 1) * MLA_NOPE_DIM],
                              qrot[:, h * MLA_ROPE_DIM:(h + 1) * MLA_ROPE_DIM]], axis=1)
        q_ref[0, h] = (qh * scale).astype(BF16)
        kh = jnp.concatenate([kvall[:, h * MLA_NOPE_DIM:(h + 1) * MLA_NOPE_DIM], krot], axis=1)
        k_ref[0, h] = kh.astype(BF16)
        vh = kvall[:, nn + h * MLA_V_DIM:nn + (h + 1) * MLA_V_DIM]
        v_ref[0, h] = jnp.concatenate([vh, ones_col], axis=1).astype(BF16)


def _mla_qkv(hcat, tabq, tabk, qn, kvn, wq, wkv):
    b, tt, _ = hcat.shape
    t = tt - ROW_BLK
    lat = lambda i: jnp.maximum(i, 1) - 1
    return pl.pallas_call(
        _mla_kernel,
        out_shape=[jax.ShapeDtypeStruct((b, MLA_HEADS, t, MLA_QK_DIM), BF16),
                   jax.ShapeDtypeStruct((b, MLA_HEADS, tt, MLA_QK_DIM), BF16),
                   jax.ShapeDtypeStruct((b, MLA_HEADS, tt, MLA_V_EXT), BF16)],
        grid=(b, tt // ROW_BLK),
        in_specs=[pl.BlockSpec((1, ROW_BLK, MLA_PROJ_EXT), lambda bi, i: (bi, i, RW_PROJ // MLA_PROJ_EXT)),
                  pl.BlockSpec((ROW_BLK, 2 * MLA_HEADS * MLA_ROPE_DIM), lambda bi, i: (lat(i), 0)),
                  pl.BlockSpec((ROW_BLK, 2 * MLA_ROPE_DIM), lambda bi, i: (i, 0)),
                  pl.BlockSpec((1, MLA_Q_LORA), lambda bi, i: (0, 0)),
                  pl.BlockSpec((1, MLA_KV_LORA), lambda bi, i: (0, 0)),
                  pl.BlockSpec(wq.shape, lambda bi, i: (0, 0)),
                  pl.BlockSpec(wkv.shape, lambda bi, i: (0, 0))],
        out_specs=[pl.BlockSpec((1, MLA_HEADS, ROW_BLK, MLA_QK_DIM), lambda bi, i: (bi, 0, lat(i), 0)),
                   pl.BlockSpec((1, MLA_HEADS, ROW_BLK, MLA_QK_DIM), lambda bi, i: (bi, 0, i, 0)),
                   pl.BlockSpec((1, MLA_HEADS, ROW_BLK, MLA_V_EXT), lambda bi, i: (bi, 0, i, 0))],
        compiler_params=_cparams(("arbitrary", "arbitrary")),
        name="mla_qkv",
    )(hcat, tabq, tabk, qn, kvn, wq, wkv)


def _attn_kernel(tk, q_ref, k_ref, v_ref, o_ref):
    q = q_ref[0, 0]
    tq = q.shape[0]
    m = jnp.full((tq, 1), -jnp.inf, F32)
    acc = jnp.zeros((tq, v_ref.shape[3]), F32)
    for j in range(k_ref.shape[2] // tk):
        s = pl.dot(q, k_ref[0, 0, j * tk:(j + 1) * tk, :], trans_b=True)
        m_new = jnp.maximum(m, jnp.max(s, axis=-1, keepdims=True))
        p = jnp.exp(s - m_new).astype(BF16)
        acc = jnp.exp(m - m_new) * acc + pl.dot(p, v_ref[0, 0, j * tk:(j + 1) * tk, :])
        m = m_new
    o_ref[0] = acc[:, :MLA_V_DIM] / acc[:, MLA_V_DIM:MLA_V_DIM + 1]


def _attn_tk(tk_total):
    for cand in (768, 1024, 512, 384, 256):
        if tk_total % cand == 0:
            return cand
    return ROW_BLK


def _attention(q, k, v):
    b, nh, t, dq = q.shape
    tk_total, dv = k.shape[2], v.shape[3]
    tq = min(ATT_TQ, t)
    return pl.pallas_call(
        functools.partial(_attn_kernel, _attn_tk(tk_total)),
        out_shape=jax.ShapeDtypeStruct((b, t, nh * MLA_V_DIM), F32),
        grid=(b, nh, t // tq),
        in_specs=[pl.BlockSpec((1, 1, tq, dq), lambda bi, h, qi: (bi, h, qi, 0)),
                  pl.BlockSpec((1, 1, tk_total, dq), lambda bi, h, qi: (bi, h, 0, 0)),
                  pl.BlockSpec((1, 1, tk_total, dv), lambda bi, h, qi: (bi, h, 0, 0))],
        out_specs=pl.BlockSpec((1, tq, MLA_V_DIM), lambda bi, h, qi: (bi, qi, h)),
        compiler_params=_cparams(("arbitrary", "arbitrary", "arbitrary")),
        name="attn",
    )(q, k, v)


def _mix_kernel(alpha, y_ref, bg_ref, g_ref, o_ref, x_ref, wout_ref, gn_ref, ln_ref, mod_ref, rt_ref, bd_ref,
                x1_ref, u_ref, aff_ref):
    bd = bd_ref[...]
    inv_n = 1.0 / RW_HEAD_DIM
    ysum = y_ref[0, 0] + y_ref[1, 0]
    yc = ysum - _mm_sel_rhs(ysum, bd) * inv_n
    var = _mm_sel_rhs(yc * yc, bd) * inv_n
    yn = yc * lax.rsqrt(var + GN_EPS) * gn_ref[0:1, :] + gn_ref[1:2, :]
    rw = yn * g_ref[0, 0] + (bg_ref[0, 0] + bg_ref[1, 0])
    mix = (pl.dot(rw.astype(BF16), wout_ref[0:RW_WIDTH, :])
           + pl.dot(o_ref[0].astype(BF16), wout_ref[RW_WIDTH:, :]))
    g1 = mod_ref[0, 0:1, :]
    sh2 = mod_ref[0, 1:2, :]
    sc2 = mod_ref[0, 2:3, :]
    x1 = _standardize(alpha * x_ref[0] + g1 * mix, LN_EPS) * ln_ref[0:1, :] + ln_ref[1:2, :]
    x1_ref[0] = x1
    u = _standardize(x1, LN_EPS) * (1.0 + sc2) + sh2
    u_ref[0] = u.astype(BF16)
    logits = _mm3(rt_ref[...], u, trans_b=True)
    ex = jnp.exp(logits - jnp.max(logits, axis=0, keepdims=True))
    aff_ref[0] = ex / jnp.sum(ex, axis=0, keepdims=True)


def _mix(alpha, y2, bg2, g2, o_mla, x, wout, gn, ln, modv, router_t, bd):
    b, t, d = x.shape
    ne = router_t.shape[0]
    row = lambda w: pl.BlockSpec((1, ROW_BLK, w), lambda bi, i: (bi, i, 0))
    two = pl.BlockSpec((2, 1, ROW_BLK, RW_WIDTH), lambda bi, i: (0, bi, i, 0))
    full = lambda a: pl.BlockSpec(a.shape, lambda bi, i: (0,) * a.ndim)
    return pl.pallas_call(
        functools.partial(_mix_kernel, alpha),
        out_shape=[jax.ShapeDtypeStruct((b, t, d), F32),
                   jax.ShapeDtypeStruct((b, t, d), BF16),
                   jax.ShapeDtypeStruct((b, ne, t), F32)],
        grid=(b, t // ROW_BLK),
        in_specs=[two, two, pl.BlockSpec((1, 1, ROW_BLK, RW_WIDTH), lambda bi, i: (0, bi, i, 0)),
                  row(MLA_HEADS * MLA_V_DIM), row(d), full(wout), full(gn), full(ln),
                  pl.BlockSpec((1, 3, d), lambda bi, i: (bi, 0, 0)), full(router_t), full(bd)],
        out_specs=[row(d), row(d), pl.BlockSpec((1, ne, ROW_BLK), lambda bi, i: (bi, 0, i))],
        compiler_params=_cparams(("arbitrary", "arbitrary")),
        name="mix",
    )(y2, bg2, g2, o_mla, x, wout, gn, ln, modv, router_t, bd)


def _topk_kernel(cap, aff_ref, w_ref):
    aff = aff_ref[0]
    ne, t = aff.shape
    bits = lax.bitcast_convert_type(aff, jnp.int32)

    def count_ge(thr):
        return jnp.sum(jnp.where(bits >= thr, 1.0, 0.0), axis=1, keepdims=True)

    def body(it, thr):
        cand = thr | lax.shift_left(jnp.int32(1), 30 - it)
        return jnp.where(count_ge(cand) >= cap, cand, thr)

    thr = lax.fori_loop(0, 31, body, jnp.zeros((ne, 1), jnp.int32))
    need = cap - jnp.sum(jnp.where(bits > thr, 1.0, 0.0), axis=1, keepdims=True)
    ri = lax.broadcasted_iota(jnp.int32, (LANES, LANES), 0)
    ci = lax.broadcasted_iota(jnp.int32, (LANES, LANES), 1)
    below = jnp.where(ri < ci, 1.0, 0.0).astype(BF16)
    carry = jnp.zeros((ne, 1), F32)
    for j in range(t // LANES):
        sl = slice(j * LANES, (j + 1) * LANES)
        bj = bits[:, sl]
        eqj = bj == thr
        eqf = jnp.where(eqj, 1.0, 0.0)
        before = pl.dot(eqf.astype(BF16), below) + carry
        take = jnp.logical_or(bj > thr, jnp.logical_and(eqj, before < need))
        w_ref[0, :, sl] = jnp.where(take, aff[:, sl], 0.0)
        carry = carry + jnp.sum(eqf, axis=1, keepdims=True)


def _topk(aff, cap):
    b, ne, t = aff.shape
    return pl.pallas_call(
        functools.partial(_topk_kernel, float(cap)),
        out_shape=jax.ShapeDtypeStruct((b, ne, t), F32),
        grid=(b,),
        in_specs=[pl.BlockSpec((1, ne, t), lambda bi: (bi, 0, 0))],
        out_specs=pl.BlockSpec((1, ne, t), lambda bi: (bi, 0, 0)),
        compiler_params=_cparams(("arbitrary",)),
        name="topk",
    )(aff)


def _moe_kernel(alpha, u_ref, wt_ref, x1_ref, wg_ref, wu_ref, wd_ref, g2_ref, ln_ref, o_ref, acc_ref):
    e = pl.program_id(2)

    @pl.when(e == 0)
    def _():
        acc_ref[...] = jnp.zeros_like(acc_ref)

    u = u_ref[0]
    hg = pl.dot(u, wg_ref[0])
    hu = pl.dot(u, wu_ref[0])
    hid = (hg * _sigmoid(hg) * hu).astype(BF16)
    ye = pl.dot(hid, wd_ref[0])
    wt = wt_ref[0]
    lane = lax.broadcasted_iota(jnp.int32, wt.shape, 1)
    wcol = jnp.sum(jnp.where(lane == e, wt, 0.0), axis=1, keepdims=True)
    acc_ref[...] += wcol * ye

    @pl.when(e == pl.num_programs(2) - 1)
    def _():
        xr = alpha * x1_ref[0] + g2_ref[0] * acc_ref[...]
        o_ref[0] = _standardize(xr, LN_EPS) * ln_ref[0:1, :] + ln_ref[1:2, :]


def _moe(alpha, u, wt, x1, wg, wu, wd, g2, ln):
    b, t, d = x1.shape
    ne, _, f = wg.shape
    tm = min(MOE_TM, t)
    row = lambda w: pl.BlockSpec((1, tm, w), lambda bi, i, e: (bi, i, 0))
    return pl.pallas_call(
        functools.partial(_moe_kernel, alpha),
        out_shape=jax.ShapeDtypeStruct((b, t, d), F32),
        grid=(b, t // tm, ne),
        in_specs=[row(d), row(ne), row(d),
                  pl.BlockSpec((1, d, f), lambda bi, i, e: (e, 0, 0)),
                  pl.BlockSpec((1, d, f), lambda bi, i, e: (e, 0, 0)),
                  pl.BlockSpec((1, f, d), lambda bi, i, e: (e, 0, 0)),
                  pl.BlockSpec((1, 1, d), lambda bi, i, e: (bi, 0, 0)),
                  pl.BlockSpec((2, d), lambda bi, i, e: (0, 0))],
        out_specs=row(d),
        scratch_shapes=[pltpu.VMEM((tm, d), F32)],
        compiler_params=_cparams(("arbitrary", "arbitrary", "arbitrary")),
        name="moe",
    )(u, wt, x1, wg, wu, wd, g2, ln)


def _rope_tables(t, tc):
    half = MLA_ROPE_DIM // 2
    inv_freq = ROPE_BASE ** (-jnp.arange(0, half, 2, dtype=F32) / half)
    pos = jnp.arange(t)
    ang_r = (pos // GRID_W).reshape(-1, 1).astype(F32) * inv_freq
    ang_c = (pos % GRID_W).reshape(-1, 1).astype(F32) * inv_freq
    cos = jnp.concatenate([jnp.cos(ang_r)] * 2 + [jnp.cos(ang_c)] * 2, axis=-1)
    sin = jnp.concatenate([-jnp.sin(ang_r), jnp.sin(ang_r), -jnp.sin(ang_c), jnp.sin(ang_c)], axis=-1)
    tabq = jnp.concatenate([jnp.tile(cos, (1, MLA_HEADS)), jnp.tile(sin, (1, MLA_HEADS))], axis=-1)
    cos_k = jnp.concatenate([jnp.ones((tc, MLA_ROPE_DIM), F32), cos], axis=0)
    sin_k = jnp.concatenate([jnp.zeros((tc, MLA_ROPE_DIM), F32), sin], axis=0)
    return tabq, jnp.concatenate([cos_k, sin_k], axis=-1)


def _pair_swap():
    q = MLA_ROPE_DIM // 4
    return np.concatenate([np.arange(q, 2 * q), np.arange(0, q), np.arange(3 * q, 4 * q), np.arange(2 * q, 3 * q)])


def kernel(x, c, ctx, c_ctx, w_ada, b_ada, w_in, rwkv_conv, rwkv_w0, rwkv_w_up, rwkv_a0, rwkv_a_up, rwkv_g_up, rwkv_k_k, rwkv_k_a, rwkv_r_k, rwkv_gn_g, rwkv_gn_b, mla_q_norm, mla_w_uq, mla_kv_norm, mla_w_uk, mla_w_uv, w_out, ln1_g, ln1_b, router, exp_w_gate, exp_w_up, exp_w_down, ln2_g, ln2_b):
    b, t, d = x.shape
    tc = ctx.shape[1]
    depth = w_ada.shape[0]
    assert depth == 1 and tc == ROW_BLK and t % ROW_BLK == 0 and t % LANES == 0
    alpha = (2.0 * depth) ** 0.25
    cap = CAPACITY_FACTOR * t // N_EXPERTS

    pad = (-(b + 1)) % 8
    cc = jnp.concatenate([c, c_ctx[None, :], jnp.zeros((pad, d), F32)], axis=0)
    mod = _ada(cc, w_ada[0], b_ada)
    sh1, sc1, g1, sh2, sc2, g2 = jnp.split(mod, 6, axis=-1)
    ss_lat = jnp.stack([sh1[:b], sc1[:b]], axis=1)
    ss_ctx = jnp.broadcast_to(jnp.stack([sh1[b], sc1[b]], axis=0)[None], (b, 2, d))
    ss = jnp.stack([ss_ctx, ss_lat], axis=1)

    swap = _pair_swap()
    kr0 = RW_PROJ + MLA_Q_LORA + MLA_KV_LORA
    w_ext = jnp.concatenate([w_in[0], w_in[0][:, kr0 + swap]], axis=1).astype(BF16)
    xcat = jnp.concatenate([ctx, x], axis=1)
    hcat = _inproj(xcat, ss, w_ext)

    head_ones = np.kron(np.eye(RW_HEADS, dtype=np.float32), np.ones((RW_HEAD_DIM, RW_HEAD_DIM), np.float32))
    bd = jnp.asarray(head_ones, BF16)
    vecs = jnp.stack([rwkv_k_k[0], rwkv_k_a[0], rwkv_r_k[0].reshape(-1)], axis=0)
    y2, bg2, gg2 = _rwkv(hcat, rwkv_conv[0], rwkv_w0[0][:, None, :], rwkv_w_up[0], rwkv_a0[0][:, None, :],
                         rwkv_a_up[0], rwkv_g_up[0], vecs, bd)

    wq = mla_w_uq[0].reshape(MLA_Q_LORA, MLA_HEADS, MLA_QK_DIM)
    wq_nope = wq[:, :, :MLA_NOPE_DIM].reshape(MLA_Q_LORA, -1)
    wq_rope = wq[:, :, MLA_NOPE_DIM:]
    wq_ext = jnp.concatenate([wq_nope, wq_rope.reshape(MLA_Q_LORA, -1),
                              wq_rope[:, :, swap].reshape(MLA_Q_LORA, -1)], axis=1).astype(BF16)
    wkv = jnp.concatenate([mla_w_uk[0], mla_w_uv[0]], axis=1).astype(BF16)
    tabq, tabk = _rope_tables(t, tc)
    q, k, v = _mla_qkv(hcat, tabq, tabk, mla_q_norm, mla_kv_norm, wq_ext, wkv)
    o_mla = _attention(q, k, v)

    gn = jnp.stack([rwkv_gn_g[0], rwkv_gn_b[0]], axis=0)
    ln1 = jnp.stack([ln1_g[0], ln1_b[0]], axis=0)
    modv = jnp.stack([g1[:b], sh2[:b], sc2[:b]], axis=1)
    x1, u, aff = _mix(alpha, y2, bg2, gg2, o_mla, x, w_out[0].astype(BF16), gn, ln1, modv, router[0].T, bd)

    wt = jnp.swapaxes(_topk(aff, cap), 1, 2)
    ln2 = jnp.stack([ln2_g[0], ln2_b[0]], axis=0)
    return _moe(alpha, u, wt, x1, exp_w_gate[0].astype(BF16), exp_w_up[0].astype(BF16),
                exp_w_down[0].astype(BF16), g2[:b, None, :], ln2)
```

```python
import functools

import jax
import jax.numpy as jnp
import numpy as np
from jax import lax
from jax.experimental import pallas as pl
from jax.experimental.pallas import tpu as pltpu

F32 = jnp.float32
BF16 = jnp.bfloat16

GRID_W = 64
RW_HEADS = 8
RW_HEAD_DIM = 64
RW_WIDTH = RW_HEADS * RW_HEAD_DIM
RW_DECAY_LORA = 64
RW_AAA_LORA = 64
RW_GATE_LORA = 128
MLA_HEADS = 4
MLA_Q_LORA = 256
MLA_KV_LORA = 256
MLA_NOPE_DIM = 128
MLA_ROPE_DIM = 64
MLA_V_DIM = 128
MLA_QK_DIM = MLA_NOPE_DIM + MLA_ROPE_DIM
MLA_V_EXT = 256
ROPE_BASE = 10000.0
N_EXPERTS = 16
CAPACITY_FACTOR = 2
LN_EPS = 1e-5
RMS_EPS = 1e-6
GN_EPS = 64e-5
RW_PROJ = 3 * RW_WIDTH + 2 * RW_DECAY_LORA + 2 * RW_AAA_LORA + RW_GATE_LORA
MLA_PROJ_EXT = MLA_Q_LORA + MLA_KV_LORA + 2 * MLA_ROPE_DIM
IN_PROJ_EXT = RW_PROJ + MLA_PROJ_EXT

ROW_BLK = 256
CHUNK = 64
LANES = 128
HALO = 8
GROUP_LANES = 256
GROUP_HEADS = GROUP_LANES // RW_HEAD_DIM
ATT_TQ = 512
MOE_WIN = 256
SUBLANES_F32 = 8
SUBLANES_BF16 = 16
MOE_GROWS = MOE_WIN + SUBLANES_F32
MOE_CROWS = MOE_WIN + SUBLANES_BF16
MOE_SEG = 2048
MOE_FIN = 512
MOE_FCHUNK = 256
VMEM_LIMIT = 48 * 1024 * 1024


def _cparams(sem):
    return pltpu.CompilerParams(dimension_semantics=sem, vmem_limit_bytes=VMEM_LIMIT)


def _split2(x):
    hi = x.astype(BF16)
    lo = (x - hi.astype(F32)).astype(BF16)
    return hi, lo


def _split3(x):
    hi = x.astype(BF16)
    r = x - hi.astype(F32)
    mid = r.astype(BF16)
    lo = (r - mid.astype(F32)).astype(BF16)
    return hi, mid, lo


def _mm3(a, b, trans_a=False, trans_b=False):
    ah, al = _split2(a)
    bh, bl = _split2(b)
    d = functools.partial(pl.dot, trans_a=trans_a, trans_b=trans_b)
    return d(ah, bh) + (d(ah, bl) + d(al, bh))


def _mm1(a, b):
    return pl.dot(a.astype(BF16), b.astype(BF16))


def _head_sums(x, group_ones, passes):
    parts = x.astype(BF16), (x - x.astype(BF16).astype(F32)).astype(BF16)
    out = []
    for gi in range(x.shape[1] // GROUP_LANES):
        ls = slice(gi * GROUP_LANES, (gi + 1) * GROUP_LANES)
        acc = pl.dot(parts[0][:, ls], group_ones)
        if passes == 2:
            acc = acc + pl.dot(parts[1][:, ls], group_ones)
        out.append(acc)
    return jnp.concatenate(out, axis=1)


def _mm_sel_rhs(a, sel):
    ah, al = _split2(a)
    return pl.dot(ah, sel) + pl.dot(al, sel)


def _mm_sel_lhs3(sel, b):
    bh, bm, bl = _split3(b)
    return pl.dot(sel, bh) + (pl.dot(sel, bm) + pl.dot(sel, bl))


def _sigmoid(x):
    return 1.0 / (1.0 + jnp.exp(-x))


def _standardize(x, eps):
    mu = jnp.mean(x, axis=-1, keepdims=True)
    xc = x - mu
    var = jnp.mean(xc * xc, axis=-1, keepdims=True)
    return xc * lax.rsqrt(var + eps)


def _ada_kernel(c_ref, w_ref, b_ref, o_ref):
    c = c_ref[...]
    s = c * _sigmoid(c)
    o_ref[...] = _mm3(s, w_ref[...]) + b_ref[...]


def _ada(cc, w, b):
    rows, d = cc.shape
    n = w.shape[1]
    return pl.pallas_call(
        _ada_kernel,
        out_shape=jax.ShapeDtypeStruct((rows, n), F32),
        grid=(n // d,),
        in_specs=[pl.BlockSpec((rows, d), lambda j: (0, 0)),
                  pl.BlockSpec((d, d), lambda j: (0, j)),
                  pl.BlockSpec((1, d), lambda j: (0, j))],
        out_specs=pl.BlockSpec((rows, d), lambda j: (0, j)),
        compiler_params=_cparams(("arbitrary",)),
        name="ada",
    )(cc, w, b)


def _inproj_kernel(x_ref, ss_ref, w_ref, o_ref):
    xn = _standardize(x_ref[0], LN_EPS)
    shift = ss_ref[0, 0, 0:1, :]
    scale = ss_ref[0, 0, 1:2, :]
    m = (xn * (1.0 + scale) + shift).astype(BF16)
    o_ref[0] = pl.dot(m, w_ref[...])


def _inproj(xcat, ss, w_ext):
    b, tt, d = xcat.shape
    n = w_ext.shape[1]
    return pl.pallas_call(
        _inproj_kernel,
        out_shape=jax.ShapeDtypeStruct((b, tt, n), F32),
        grid=(b, tt // ROW_BLK),
        in_specs=[pl.BlockSpec((1, ROW_BLK, d), lambda bi, i: (bi, i, 0)),
                  pl.BlockSpec((1, 1, 2, d), lambda bi, i: (bi, jnp.minimum(i, 1), 0, 0)),
                  pl.BlockSpec((d, n), lambda bi, i: (0, 0))],
        out_specs=pl.BlockSpec((1, ROW_BLK, n), lambda bi, i: (bi, i, 0)),
        compiler_params=_cparams(("arbitrary", "arbitrary")),
        name="inproj",
    )(xcat, ss, w_ext)


def _rwkv_block(d, i, nblk):
    return jnp.where(d == 0, i, jnp.where(i == 0, 0, nblk - i))


def _rwkv_kernel(hm_ref, hp_ref, hn_ref, conv_ref, w0_ref, wup_ref, a0_ref, aup_ref, gup_ref, vec_ref, bd_ref,
                 y_ref, bg_ref, g_ref,
                 s_ref, rt_ref, kt_ref, at_ref, bt_ref, v_ref, a2_ref, k2_ref, gc_ref, w_ref, ub_ref, mra_ref, yv_ref):
    d = pl.program_id(0)
    i = pl.program_id(2)
    nblk = pl.num_programs(2)
    pb = _rwkv_block(d, i, nblk)
    fwd = d == 0
    w3 = 3 * RW_WIDTH

    @pl.when(i == 0)
    def _():
        s_ref[...] = jnp.zeros_like(s_ref)

    rkv = hm_ref[0, :, 0:w3]
    first = jnp.logical_or(pb == 0, pb == 1)
    last = jnp.logical_or(pb == 0, pb == nblk - 1)
    prev_row = jnp.where(first, 0.0, hp_ref[0, HALO - 1:HALO, :])
    next_row = jnp.where(last, 0.0, hn_ref[0, 0:1, :])
    rows = lax.broadcasted_iota(jnp.int32, (ROW_BLK, 1), 0)
    up = jnp.where(rows == 0, prev_row, pltpu.roll(rkv, 1, 0))
    dn = jnp.where(rows == ROW_BLK - 1, next_row, pltpu.roll(rkv, ROW_BLK - 1, 0))
    rkv = conv_ref[0:1, :] * up + conv_ref[1:2, :] * rkv + conv_ref[2:3, :] * dn
    r = rkv[:, 0:RW_WIDTH]
    k = rkv[:, RW_WIDTH:2 * RW_WIDTH]
    v = rkv[:, 2 * RW_WIDTH:w3]

    wdn2 = hm_ref[0, :, w3:w3 + 2 * RW_DECAY_LORA]
    adn2 = hm_ref[0, :, w3 + 2 * RW_DECAY_LORA:w3 + 2 * RW_DECAY_LORA + 2 * RW_AAA_LORA]
    gdn = hm_ref[0, :, RW_PROJ - RW_GATE_LORA:RW_PROJ]
    wdn = jnp.where(fwd, wdn2[:, :RW_DECAY_LORA], wdn2[:, RW_DECAY_LORA:])
    adn = jnp.where(fwd, adn2[:, :RW_AAA_LORA], adn2[:, RW_AAA_LORA:])

    k_k = vec_ref[0:1, :]
    k_a = vec_ref[1:2, :]
    r_k = vec_ref[2:3, :]
    bd = bd_ref[...]

    z = w0_ref[0] + _mm1(jnp.tanh(wdn), wup_ref[0])
    ld = (-float(np.exp(-0.5))) * _sigmoid(z)
    a = _sigmoid(a0_ref[0] + _mm1(adn, aup_ref[0]))
    g = _mm1(_sigmoid(gdn), gup_ref[...])
    kd = k * (1.0 + (a - 1.0) * k_a)
    kkr = k * k_k
    kk = kkr * lax.rsqrt(_head_sums(kkr * kkr, bd, 2) + 1e-12)
    g_ref[0, 0] = g
    bg_ref[0, 0] = _head_sums(r * kd * r_k, bd, 1) * v * g

    ri = lax.broadcasted_iota(jnp.int32, (ROW_BLK, ROW_BLK), 0)
    ci = lax.broadcasted_iota(jnp.int32, (ROW_BLK, ROW_BLK), 1)
    same = (ri & -CHUNK) == (ci & -CHUNK)
    sgn = jnp.where(fwd, 1, -1)
    tri = jnp.logical_and(same, (ri - ci) * sgn >= 0)
    cl = _mm_sel_lhs3(jnp.where(tri, 1.0, 0.0).astype(BF16), ld)
    tot = jnp.concatenate(
        [jnp.broadcast_to(jnp.where(fwd, cl[c * CHUNK + CHUNK - 1:(c + 1) * CHUNK], cl[c * CHUNK:c * CHUNK + 1]),
                          (CHUNK, RW_WIDTH)) for c in range(ROW_BLK // CHUNK)], axis=0)
    e_neg = jnp.exp(-cl)
    e_rem = jnp.exp(tot - cl)
    ka = kk * a
    rt_ref[...] = r * jnp.exp(cl)
    kt_ref[...] = kd * e_neg
    at_ref[...] = -ka * e_neg
    bt_ref[...] = kk * jnp.exp(cl - ld)
    v_ref[...] = v
    a2_ref[...] = -ka * e_rem
    k2_ref[...] = kd * e_rem
    gc_ref[...] = jnp.exp(tot)

    r4 = lax.broadcasted_iota(jnp.int32, (GROUP_LANES, GROUP_LANES), 0) & -RW_HEAD_DIM
    c4 = lax.broadcasted_iota(jnp.int32, (GROUP_LANES, GROUP_LANES), 1) & -RW_HEAD_DIM
    same_head = r4 == c4
    head_mask = jnp.where(same_head, 1.0, 0.0).astype(BF16)
    ii = lax.broadcasted_iota(jnp.int32, (CHUNK, GROUP_LANES), 0)
    jj = lax.broadcasted_iota(jnp.int32, (CHUNK, GROUP_LANES), 1) & (CHUNK - 1)
    strict = (ii - jj) * sgn > 0
    incl = (ii - jj) * sgn >= 0
    eye = jnp.where(ii == jj, 1.0, 0.0)

    def bdiag(x):
        xb = x.astype(BF16)
        return jnp.concatenate([xb] * GROUP_HEADS, axis=0) * head_mask

    def stack(top, bottom):
        return jnp.concatenate([top, bottom], axis=0).astype(BF16)

    n_chunks = ROW_BLK // CHUNK
    chains = [(c, gi) for c in range(n_chunks) for gi in range(RW_WIDTH // GROUP_LANES)]

    def sl(ref, c, gi):
        return ref[c * CHUNK:(c + 1) * CHUNK, gi * GROUP_LANES:(gi + 1) * GROUP_LANES]

    pw, tt, lbkv = {}, {}, {}
    for key in chains:
        c, gi = key
        rows, lanes = slice(c * CHUNK, (c + 1) * CHUNK), slice(gi * GROUP_LANES, (gi + 1) * GROUP_LANES)
        lhs = stack(sl(bt_ref, *key), sl(rt_ref, *key))
        ga = pl.dot(lhs, bdiag(sl(at_ref, *key)), trans_b=True)
        gk = pl.dot(lhs, bdiag(sl(kt_ref, *key)), trans_b=True)
        mra_ref[rows, lanes] = jnp.where(incl, ga[CHUNK:], 0.0)
        pw[key] = jnp.where(strict, ga[:CHUNK], 0.0)
        tt[key] = eye + pw[key]
        lm = stack(jnp.where(strict, gk[:CHUNK], 0.0), jnp.where(incl, gk[CHUNK:], 0.0))
        lv = pl.dot(lm, bdiag(sl(v_ref, *key)))
        lbkv[key] = lv[:CHUNK]
        yv_ref[rows, lanes] = lv[CHUNK:]
    for key in chains:
        pw[key] = pl.dot(pw[key].astype(BF16), bdiag(pw[key]))
    for _ in range(4):
        for key in chains:
            st = pl.dot(stack(pw[key], tt[key]), bdiag(pw[key]))
            pw[key] = st[:CHUNK]
            tt[key] = tt[key] + st[CHUNK:]
    for key in chains:
        c, gi = key
        rows, lanes = slice(c * CHUNK, (c + 1) * CHUNK), slice(gi * GROUP_LANES, (gi + 1) * GROUP_LANES)
        t_inv = (tt[key] + pl.dot(tt[key].astype(BF16), bdiag(pw[key]))).astype(BF16)
        w_ref[rows, lanes] = pl.dot(t_inv, bdiag(sl(bt_ref, *key)))
        ub_ref[rows, lanes] = pl.dot(t_inv, bdiag(lbkv[key]))

    for c in range(n_chunks):
        cc = jnp.where(fwd, c, n_chunks - 1 - c)
        r0 = pl.multiple_of(cc * CHUNK, CHUNK)
        rs = pl.ds(r0, CHUNK)
        for gi in range(RW_WIDTH // GROUP_LANES):
            ls = slice(gi * GROUP_LANES, (gi + 1) * GROUP_LANES)
            s0 = s_ref[:, ls]
            s_hi, s_lo = _split2(s0)
            lhs = stack(w_ref[rs, ls], rt_ref[rs, ls])
            ws = pl.dot(lhs, bdiag(s_hi), trans_b=True) + pl.dot(lhs, bdiag(s_lo), trans_b=True)
            u = ub_ref[rs, ls] + ws[:CHUNK]
            y_ref[0, 0, rs, ls] = ws[CHUNK:] + yv_ref[rs, ls] + pl.dot(mra_ref[rs, ls].astype(BF16), bdiag(u))
            uv = jnp.concatenate([u, v_ref[rs, ls]], axis=0)
            ak2 = jnp.concatenate([a2_ref[rs, ls], k2_ref[rs, ls]], axis=0)
            full = jnp.where(same_head, pl.dot(uv, ak2, trans_a=True), 0.0)
            inc = full[0:RW_HEAD_DIM]
            for hh in range(1, GROUP_HEADS):
                inc = inc + full[hh * RW_HEAD_DIM:(hh + 1) * RW_HEAD_DIM]
            s_ref[:, ls] = s0 * gc_ref[pl.ds(r0, 1), ls] + inc


def _rwkv(hcat, conv, w0, wup, a0, aup, gup, vecs, bd):
    b, tt, _ = hcat.shape
    nblk = tt // ROW_BLK
    t = tt - ROW_BLK
    w3 = 3 * RW_WIDTH
    hb = ROW_BLK // HALO

    def pbf(d, i):
        return _rwkv_block(d, i, nblk)

    def obf(d, i):
        return _rwkv_block(d, jnp.maximum(i, 1), nblk) - 1

    out_spec = pl.BlockSpec((1, 1, ROW_BLK, RW_WIDTH), lambda d, bi, i: (d, bi, obf(d, i), 0))
    full = lambda shape: pl.BlockSpec(shape, lambda d, bi, i: (0,) * len(shape))
    perdir = lambda shape: pl.BlockSpec((1,) + shape, lambda d, bi, i: (d,) + (0,) * len(shape))
    scr = lambda: pltpu.VMEM((ROW_BLK, RW_WIDTH), F32)
    return pl.pallas_call(
        _rwkv_kernel,
        out_shape=[jax.ShapeDtypeStruct((2, b, t, RW_WIDTH), F32)] * 3,
        grid=(2, b, nblk),
        in_specs=[
            pl.BlockSpec((1, ROW_BLK, RW_PROJ), lambda d, bi, i: (bi, pbf(d, i), 0)),
            pl.BlockSpec((1, HALO, w3), lambda d, bi, i: (bi, jnp.maximum(pbf(d, i) * hb - 1, 0), 0)),
            pl.BlockSpec((1, HALO, w3), lambda d, bi, i: (bi, jnp.minimum((pbf(d, i) + 1) * hb, nblk * hb - 1), 0)),
            full((3, w3)),
            perdir((1, RW_WIDTH)), perdir((RW_DECAY_LORA, RW_WIDTH)),
            perdir((1, RW_WIDTH)), perdir((RW_AAA_LORA, RW_WIDTH)),
            full((RW_GATE_LORA, RW_WIDTH)), full((3, RW_WIDTH)), full((GROUP_LANES, GROUP_LANES)),
        ],
        out_specs=[out_spec, out_spec, out_spec],
        scratch_shapes=[pltpu.VMEM((RW_HEAD_DIM, RW_WIDTH), F32)] + [scr() for _ in range(12)],
        compiler_params=_cparams(("arbitrary", "arbitrary", "arbitrary")),
        name="rwkv",
    )(hcat, hcat, hcat, conv, w0, wup, a0, aup, gup, vecs, bd)


def _mla_kernel(h_ref, tq_ref, tk_ref, qn_ref, kvn_ref, wq_ref, wkv_ref, q_ref, k_ref, v_ref):
    hq = h_ref[0, :, 0:MLA_Q_LORA]
    hkv = h_ref[0, :, MLA_Q_LORA:MLA_Q_LORA + MLA_KV_LORA]
    kr = h_ref[0, :, MLA_Q_LORA + MLA_KV_LORA:MLA_Q_LORA + MLA_KV_LORA + MLA_ROPE_DIM]
    krs = h_ref[0, :, MLA_Q_LORA + MLA_KV_LORA + MLA_ROPE_DIM:MLA_PROJ_EXT]

    def rms(x, gain):
        return x * lax.rsqrt(jnp.mean(x * x, axis=-1, keepdims=True) + RMS_EPS) * gain

    qall = pl.dot(rms(hq, qn_ref[...]).astype(BF16), wq_ref[...])
    kvall = pl.dot(rms(hkv, kvn_ref[...]).astype(BF16), wkv_ref[...])
    nn = MLA_HEADS * MLA_NOPE_DIM
    nr = MLA_HEADS * MLA_ROPE_DIM
    qrot = qall[:, nn:nn + nr] * tq_ref[:, 0:nr] + qall[:, nn + nr:nn + 2 * nr] * tq_ref[:, nr:2 * nr]
    krot = kr * tk_ref[:, 0:MLA_ROPE_DIM] + krs * tk_ref[:, MLA_ROPE_DIM:2 * MLA_ROPE_DIM]
    scale = MLA_QK_DIM ** -0.5
    ones_col = jnp.where(lax.broadcasted_iota(jnp.int32, (ROW_BLK, MLA_V_EXT - MLA_V_DIM), 1) == 0, 1.0, 0.0)
    for h in range(MLA_HEADS):
        qh = jnp.concatenate([qall[:, h * MLA_NOPE_DIM:(h + 1) * MLA_NOPE_DIM],
                              qrot[:, h * MLA_ROPE_DIM:(h + 1) * MLA_ROPE_DIM]], axis=1)
        q_ref[0, h] = (qh * scale).astype(BF16)
        kh = jnp.concatenate([kvall[:, h * MLA_NOPE_DIM:(h + 1) * MLA_NOPE_DIM], krot], axis=1)
        k_ref[0, h] = kh.astype(BF16)
        vh = kvall[:, nn + h * MLA_V_DIM:nn + (h + 1) * MLA_V_DIM]
        v_ref[0, h] = jnp.concatenate([vh, ones_col], axis=1).astype(BF16)


def _mla_qkv(hcat, tabq, tabk, qn, kvn, wq, wkv):
    b, tt, _ = hcat.shape
    t = tt - ROW_BLK
    lat = lambda i: jnp.maximum(i, 1) - 1
    return pl.pallas_call(
        _mla_kernel,
        out_shape=[jax.ShapeDtypeStruct((b, MLA_HEADS, t, MLA_QK_DIM), BF16),
                   jax.ShapeDtypeStruct((b, MLA_HEADS, tt, MLA_QK_DIM), BF16),
                   jax.ShapeDtypeStruct((b, MLA_HEADS, tt, MLA_V_EXT), BF16)],
        grid=(b, tt // ROW_BLK),
        in_specs=[pl.BlockSpec((1, ROW_BLK, MLA_PROJ_EXT), lambda bi, i: (bi, i, RW_PROJ // MLA_PROJ_EXT)),
                  pl.BlockSpec((ROW_BLK, 2 * MLA_HEADS * MLA_ROPE_DIM), lambda bi, i: (lat(i), 0)),
                  pl.BlockSpec((ROW_BLK, 2 * MLA_ROPE_DIM), lambda bi, i: (i, 0)),
                  pl.BlockSpec((1, MLA_Q_LORA), lambda bi, i: (0, 0)),
                  pl.BlockSpec((1, MLA_KV_LORA), lambda bi, i: (0, 0)),
                  pl.BlockSpec(wq.shape, lambda bi, i: (0, 0)),
                  pl.BlockSpec(wkv.shape, lambda bi, i: (0, 0))],
        out_specs=[pl.BlockSpec((1, MLA_HEADS, ROW_BLK, MLA_QK_DIM), lambda bi, i: (bi, 0, lat(i), 0)),
                   pl.BlockSpec((1, MLA_HEADS, ROW_BLK, MLA_QK_DIM), lambda bi, i: (bi, 0, i, 0)),
                   pl.BlockSpec((1, MLA_HEADS, ROW_BLK, MLA_V_EXT), lambda bi, i: (bi, 0, i, 0))],
        compiler_params=_cparams(("arbitrary", "arbitrary")),
        name="mla_qkv",
    )(hcat, tabq, tabk, qn, kvn, wq, wkv)


def _attn_kernel(tk, q_ref, k_ref, v_ref, o_ref):
    q = q_ref[0, 0]
    tq = q.shape[0]
    m = jnp.full((tq, 1), -jnp.inf, F32)
    acc = jnp.zeros((tq, v_ref.shape[3]), F32)
    for j in range(k_ref.shape[2] // tk):
        s = pl.dot(q, k_ref[0, 0, j * tk:(j + 1) * tk, :], trans_b=True)
        m_new = jnp.maximum(m, jnp.max(s, axis=-1, keepdims=True))
        p = jnp.exp(s - m_new).astype(BF16)
        acc = jnp.exp(m - m_new) * acc + pl.dot(p, v_ref[0, 0, j * tk:(j + 1) * tk, :])
        m = m_new
    o_ref[0] = acc[:, :MLA_V_DIM] / acc[:, MLA_V_DIM:MLA_V_DIM + 1]


def _attn_tk(tk_total):
    for cand in (768, 1024, 512, 384, 256):
        if tk_total % cand == 0:
            return cand
    return ROW_BLK


def _attention(q, k, v):
    b, nh, t, dq = q.shape
    tk_total, dv = k.shape[2], v.shape[3]
    tq = min(ATT_TQ, t)
    return pl.pallas_call(
        functools.partial(_attn_kernel, _attn_tk(tk_total)),
        out_shape=jax.ShapeDtypeStruct((b, t, nh * MLA_V_DIM), F32),
        grid=(b, nh, t // tq),
        in_specs=[pl.BlockSpec((1, 1, tq, dq), lambda bi, h, qi: (bi, h, qi, 0)),
                  pl.BlockSpec((1, 1, tk_total, dq), lambda bi, h, qi: (bi, h, 0, 0)),
                  pl.BlockSpec((1, 1, tk_total, dv), lambda bi, h, qi: (bi, h, 0, 0))],
        out_specs=pl.BlockSpec((1, tq, MLA_V_DIM), lambda bi, h, qi: (bi, qi, h)),
        compiler_params=_cparams(("arbitrary", "arbitrary", "arbitrary")),
        name="attn",
    )(q, k, v)


def _mix_kernel(alpha, y_ref, bg_ref, g_ref, o_ref, x_ref, wout_ref, gn_ref, ln_ref, mod_ref, rt_ref, bd_ref,
                x1_ref, u_ref, aff_ref):
    bd = bd_ref[...]
    inv_n = 1.0 / RW_HEAD_DIM
    ysum = y_ref[0, 0] + y_ref[1, 0]
    yc = ysum - _mm_sel_rhs(ysum, bd) * inv_n
    var = _mm_sel_rhs(yc * yc, bd) * inv_n
    yn = yc * lax.rsqrt(var + GN_EPS) * gn_ref[0:1, :] + gn_ref[1:2, :]
    rw = yn * g_ref[0, 0] + (bg_ref[0, 0] + bg_ref[1, 0])
    mix = (pl.dot(rw.astype(BF16), wout_ref[0:RW_WIDTH, :])
           + pl.dot(o_ref[0].astype(BF16), wout_ref[RW_WIDTH:, :]))
    g1 = mod_ref[0, 0:1, :]
    sh2 = mod_ref[0, 1:2, :]
    sc2 = mod_ref[0, 2:3, :]
    x1 = _standardize(alpha * x_ref[0] + g1 * mix, LN_EPS) * ln_ref[0:1, :] + ln_ref[1:2, :]
    x1_ref[0] = x1
    u = _standardize(x1, LN_EPS) * (1.0 + sc2) + sh2
    u_ref[0] = u.astype(BF16)
    logits = _mm3(rt_ref[...], u, trans_b=True)
    ex = jnp.exp(logits - jnp.max(logits, axis=0, keepdims=True))
    aff_ref[0] = ex / jnp.sum(ex, axis=0, keepdims=True)


def _mix(alpha, y2, bg2, g2, o_mla, x, wout, gn, ln, modv, router_t, bd):
    b, t, d = x.shape
    ne = router_t.shape[0]
    row = lambda w: pl.BlockSpec((1, ROW_BLK, w), lambda bi, i: (bi, i, 0))
    two = pl.BlockSpec((2, 1, ROW_BLK, RW_WIDTH), lambda bi, i: (0, bi, i, 0))
    full = lambda a: pl.BlockSpec(a.shape, lambda bi, i: (0,) * a.ndim)
    return pl.pallas_call(
        functools.partial(_mix_kernel, alpha),
        out_shape=[jax.ShapeDtypeStruct((b, t, d), F32),
                   jax.ShapeDtypeStruct((b, t, d), BF16),
                   jax.ShapeDtypeStruct((b, ne, t), F32)],
        grid=(b, t // ROW_BLK),
        in_specs=[two, two, pl.BlockSpec((1, 1, ROW_BLK, RW_WIDTH), lambda bi, i: (0, bi, i, 0)),
                  row(MLA_HEADS * MLA_V_DIM), row(d), full(wout), full(gn), full(ln),
                  pl.BlockSpec((1, 3, d), lambda bi, i: (bi, 0, 0)), full(router_t), full(bd)],
        out_specs=[row(d), row(d), pl.BlockSpec((1, ne, ROW_BLK), lambda bi, i: (bi, 0, i))],
        compiler_params=_cparams(("arbitrary", "arbitrary")),
        name="mix",
    )(y2, bg2, g2, o_mla, x, wout, gn, ln, modv, router_t, bd)


def _topk_kernel(cap, aff_ref, w_ref, pos_ref, cnt_ref):
    aff = aff_ref[0]
    ne, t = aff.shape
    bits = lax.bitcast_convert_type(aff, jnp.int32)

    def count_ge(thr):
        return jnp.sum(jnp.where(bits >= thr, 1.0, 0.0), axis=1, keepdims=True)

    def body(it, thr):
        cand = thr | lax.shift_left(jnp.int32(1), 30 - it)
        return jnp.where(count_ge(cand) >= cap, cand, thr)

    thr = lax.fori_loop(0, 31, body, jnp.zeros((ne, 1), jnp.int32))
    need = cap - jnp.sum(jnp.where(bits > thr, 1.0, 0.0), axis=1, keepdims=True)
    ri = lax.broadcasted_iota(jnp.int32, (LANES, LANES), 0)
    ci = lax.broadcasted_iota(jnp.int32, (LANES, LANES), 1)
    below = jnp.where(ri < ci, 1.0, 0.0).astype(BF16)
    ties = jnp.zeros((ne, 1), F32)
    taken = jnp.zeros((ne, 1), F32)
    for j in range(t // LANES):
        sl = slice(j * LANES, (j + 1) * LANES)
        bj = bits[:, sl]
        eqj = bj == thr
        eqf = jnp.where(eqj, 1.0, 0.0)
        before = pl.dot(eqf.astype(BF16), below) + ties
        take = jnp.logical_or(bj > thr, jnp.logical_and(eqj, before < need))
        takef = jnp.where(take, 1.0, 0.0)
        w_ref[0, :, sl] = jnp.where(take, aff[:, sl], 0.0)
        pos_ref[0, :, sl] = jnp.where(take, pl.dot(takef.astype(BF16), below) + taken, -1.0)
        cnt_ref[0, :, j:j + 1] = taken
        ties = ties + jnp.sum(eqf, axis=1, keepdims=True)
        taken = taken + jnp.sum(takef, axis=1, keepdims=True)


def _topk(aff, cap):
    b, ne, t = aff.shape
    spec = pl.BlockSpec((1, ne, t), lambda bi: (bi, 0, 0))
    return pl.pallas_call(
        functools.partial(_topk_kernel, float(cap)),
        out_shape=[jax.ShapeDtypeStruct((b, ne, t), F32), jax.ShapeDtypeStruct((b, ne, t), F32),
                   jax.ShapeDtypeStruct((b, ne, t // LANES), F32)],
        grid=(b,),
        in_specs=[spec],
        out_specs=[spec, spec, pl.BlockSpec((1, ne, t // LANES), lambda bi: (bi, 0, 0))],
        compiler_params=_cparams(("arbitrary",)),
        name="topk",
    )(aff)


def _moe_ffn_kernel(cap, ws_ref, u_ref, pos_ref, wg_ref, wu_ref, wd_ref, ye_ref, xe_ref, acc_ref):
    bi, e, q = pl.program_id(0), pl.program_id(1), pl.program_id(2)
    ne, nq = pl.num_programs(1), pl.num_programs(2)
    nwin = u_ref.shape[1] // MOE_WIN
    f = wg_ref.shape[2]

    @pl.when(q == 0)
    def _():
        xe_ref[...] = jnp.zeros_like(xe_ref)

    rows = lax.broadcasted_iota(jnp.int32, (MOE_GROWS, 1), 0).astype(F32)
    for wl in range(nwin):
        start = pl.multiple_of(ws_ref[(bi * ne + e) * (nq * nwin) + q * nwin + wl] & -SUBLANES_F32, SUBLANES_F32)
        rank = pos_ref[0, pl.ds(e, 1), wl * MOE_WIN:(wl + 1) * MOE_WIN] - start.astype(F32)
        onehot = jnp.where(rank == rows, 1.0, 0.0).astype(BF16)
        xe_ref[pl.ds(start, MOE_GROWS), :] += pl.dot(onehot, u_ref[0, wl * MOE_WIN:(wl + 1) * MOE_WIN, :])

    @pl.when(q == nq - 1)
    def _():
        xb = xe_ref[0:cap, :].astype(BF16)
        for c in range(f // MOE_FCHUNK):
            cs = slice(c * MOE_FCHUNK, (c + 1) * MOE_FCHUNK)
            hg = pl.dot(xb, wg_ref[0, :, cs])
            hu = pl.dot(xb, wu_ref[0, :, cs])
            part = pl.dot((hg * _sigmoid(hg) * hu).astype(BF16), wd_ref[0, cs, :])
            if c == 0:
                acc_ref[...] = part
            else:
                acc_ref[...] += part
        ye_ref[0, 0, 0:cap, :] = acc_ref[...].astype(BF16)
        ye_ref[0, 0, cap:, :] = jnp.zeros((ye_ref.shape[2] - cap, ye_ref.shape[3]), BF16)


def _moe_ffn(ws, u, pos, wg, wu, wd, cap):
    b, t, d = u.shape
    ne, _, f = wg.shape
    seg = min(MOE_SEG, t)
    cap_pad = cap + MOE_CROWS
    grid_spec = pltpu.PrefetchScalarGridSpec(
        num_scalar_prefetch=1,
        grid=(b, ne, t // seg),
        in_specs=[pl.BlockSpec((1, seg, d), lambda bi, e, q, ws: (bi, q, 0)),
                  pl.BlockSpec((1, ne, seg), lambda bi, e, q, ws: (bi, 0, q)),
                  pl.BlockSpec((1, d, f), lambda bi, e, q, ws: (e, 0, 0)),
                  pl.BlockSpec((1, d, f), lambda bi, e, q, ws: (e, 0, 0)),
                  pl.BlockSpec((1, f, d), lambda bi, e, q, ws: (e, 0, 0))],
        out_specs=pl.BlockSpec((1, 1, cap_pad, d), lambda bi, e, q, ws: (bi, e, 0, 0)),
        scratch_shapes=[pltpu.VMEM((cap_pad, d), F32), pltpu.VMEM((cap, d), F32)])
    return pl.pallas_call(
        functools.partial(_moe_ffn_kernel, cap),
        out_shape=jax.ShapeDtypeStruct((b, ne, cap_pad, d), BF16),
        grid_spec=grid_spec,
        compiler_params=_cparams(("arbitrary", "arbitrary", "arbitrary")),
        name="moe_ffn",
    )(ws, u, pos, wg, wu, wd)


def _moe_combine_kernel(alpha, ne, ws_ref, ye_ref, pos_ref, wt_ref, x1_ref, g2_ref, ln_ref, o_ref, acc_ref):
    bi, qq, s = pl.program_id(0), pl.program_id(1), pl.program_id(2)
    nwin = acc_ref.shape[0] // MOE_WIN
    total_win = pl.num_programs(1) * nwin

    @pl.when(s == 0)
    def _():
        acc_ref[...] = jnp.zeros_like(acc_ref)

    @pl.when(s < ne)
    def _():
        lane = lax.broadcasted_iota(jnp.int32, (MOE_WIN, ne), 1)
        cols = lax.broadcasted_iota(jnp.int32, (1, MOE_CROWS), 1).astype(F32)
        for wl in range(nwin):
            ts = slice(wl * MOE_WIN, (wl + 1) * MOE_WIN)
            start = pl.multiple_of(ws_ref[(bi * ne + s) * total_win + qq * nwin + wl] & -SUBLANES_BF16, SUBLANES_BF16)
            rank = jnp.sum(jnp.where(lane == s, pos_ref[0, ts, :], 0.0), axis=1, keepdims=True) - start.astype(F32)
            gate = jnp.sum(jnp.where(lane == s, wt_ref[0, ts, :], 0.0), axis=1, keepdims=True)
            onehot = jnp.where(rank == cols, 1.0, 0.0).astype(BF16)
            acc_ref[ts, :] += gate * pl.dot(onehot, ye_ref[0, 0, pl.ds(start, MOE_CROWS), :])

    @pl.when(s >= ne)
    def _():
        r0 = pl.multiple_of((s - ne) * o_ref.shape[1], o_ref.shape[1])
        xr = alpha * x1_ref[0] + g2_ref[0] * acc_ref[pl.ds(r0, o_ref.shape[1]), :]
        o_ref[0] = _standardize(xr, LN_EPS) * ln_ref[0:1, :] + ln_ref[1:2, :]


def _moe_combine(alpha, ws, ye, pos_t, wt_t, x1, g2, ln):
    b, t, d = x1.shape
    ne, cap_pad = ye.shape[1], ye.shape[2]
    seg = min(MOE_SEG, t)
    fin = min(MOE_FIN, seg)
    nfin = seg // fin
    fblk = lambda bi, qq, s, ws: (bi, qq * nfin + jnp.maximum(s - ne, 0), 0)
    grid_spec = pltpu.PrefetchScalarGridSpec(
        num_scalar_prefetch=1,
        grid=(b, t // seg, ne + nfin),
        in_specs=[pl.BlockSpec((1, 1, cap_pad, d), lambda bi, qq, s, ws: (bi, jnp.minimum(s, ne - 1), 0, 0)),
                  pl.BlockSpec((1, seg, ne), lambda bi, qq, s, ws: (bi, qq, 0)),
                  pl.BlockSpec((1, seg, ne), lambda bi, qq, s, ws: (bi, qq, 0)),
                  pl.BlockSpec((1, fin, d), fblk),
                  pl.BlockSpec((1, 1, d), lambda bi, qq, s, ws: (bi, 0, 0)),
                  pl.BlockSpec((2, d), lambda bi, qq, s, ws: (0, 0))],
        out_specs=pl.BlockSpec((1, fin, d), fblk),
        scratch_shapes=[pltpu.VMEM((seg, d), F32)])
    return pl.pallas_call(
        functools.partial(_moe_combine_kernel, alpha, ne),
        out_shape=jax.ShapeDtypeStruct((b, t, d), F32),
        grid_spec=grid_spec,
        compiler_params=_cparams(("arbitrary", "arbitrary", "arbitrary")),
        name="moe_combine",
    )(ws, ye, pos_t, wt_t, x1, g2, ln)


def _rope_tables(t, tc):
    half = MLA_ROPE_DIM // 2
    inv_freq = ROPE_BASE ** (-jnp.arange(0, half, 2, dtype=F32) / half)
    pos = jnp.arange(t)
    ang_r = (pos // GRID_W).reshape(-1, 1).astype(F32) * inv_freq
    ang_c = (pos % GRID_W).reshape(-1, 1).astype(F32) * inv_freq
    cos = jnp.concatenate([jnp.cos(ang_r)] * 2 + [jnp.cos(ang_c)] * 2, axis=-1)
    sin = jnp.concatenate([-jnp.sin(ang_r), jnp.sin(ang_r), -jnp.sin(ang_c), jnp.sin(ang_c)], axis=-1)
    tabq = jnp.concatenate([jnp.tile(cos, (1, MLA_HEADS)), jnp.tile(sin, (1, MLA_HEADS))], axis=-1)
    cos_k = jnp.concatenate([jnp.ones((tc, MLA_ROPE_DIM), F32), cos], axis=0)
    sin_k = jnp.concatenate([jnp.zeros((tc, MLA_ROPE_DIM), F32), sin], axis=0)
    return tabq, jnp.concatenate([cos_k, sin_k], axis=-1)


def _pair_swap():
    q = MLA_ROPE_DIM // 4
    return np.concatenate([np.arange(q, 2 * q), np.arange(0, q), np.arange(3 * q, 4 * q), np.arange(2 * q, 3 * q)])


def kernel(x, c, ctx, c_ctx, w_ada, b_ada, w_in, rwkv_conv, rwkv_w0, rwkv_w_up, rwkv_a0, rwkv_a_up, rwkv_g_up, rwkv_k_k, rwkv_k_a, rwkv_r_k, rwkv_gn_g, rwkv_gn_b, mla_q_norm, mla_w_uq, mla_kv_norm, mla_w_uk, mla_w_uv, w_out, ln1_g, ln1_b, router, exp_w_gate, exp_w_up, exp_w_down, ln2_g, ln2_b):
    b, t, d = x.shape
    tc = ctx.shape[1]
    depth = w_ada.shape[0]
    assert depth == 1 and tc == ROW_BLK and t % ROW_BLK == 0 and t % LANES == 0
    alpha = (2.0 * depth) ** 0.25
    cap = CAPACITY_FACTOR * t // N_EXPERTS

    pad = (-(b + 1)) % 8
    cc = jnp.concatenate([c, c_ctx[None, :], jnp.zeros((pad, d), F32)], axis=0)
    mod = _ada(cc, w_ada[0], b_ada)
    sh1, sc1, g1, sh2, sc2, g2 = jnp.split(mod, 6, axis=-1)
    ss_lat = jnp.stack([sh1[:b], sc1[:b]], axis=1)
    ss_ctx = jnp.broadcast_to(jnp.stack([sh1[b], sc1[b]], axis=0)[None], (b, 2, d))
    ss = jnp.stack([ss_ctx, ss_lat], axis=1)

    swap = _pair_swap()
    kr0 = RW_PROJ + MLA_Q_LORA + MLA_KV_LORA
    w_ext = jnp.concatenate([w_in[0], w_in[0][:, kr0 + swap]], axis=1).astype(BF16)
    xcat = jnp.concatenate([ctx, x], axis=1)
    hcat = _inproj(xcat, ss, w_ext)

    head_ones = np.kron(np.eye(RW_HEADS, dtype=np.float32), np.ones((RW_HEAD_DIM, RW_HEAD_DIM), np.float32))
    bd = jnp.asarray(head_ones, BF16)
    vecs = jnp.stack([rwkv_k_k[0], rwkv_k_a[0], rwkv_r_k[0].reshape(-1)], axis=0)
    y2, bg2, gg2 = _rwkv(hcat, rwkv_conv[0], rwkv_w0[0][:, None, :], rwkv_w_up[0], rwkv_a0[0][:, None, :],
                         rwkv_a_up[0], rwkv_g_up[0], vecs, bd[:GROUP_LANES, :GROUP_LANES])

    wq = mla_w_uq[0].reshape(MLA_Q_LORA, MLA_HEADS, MLA_QK_DIM)
    wq_nope = wq[:, :, :MLA_NOPE_DIM].reshape(MLA_Q_LORA, -1)
    wq_rope = wq[:, :, MLA_NOPE_DIM:]
    wq_ext = jnp.concatenate([wq_nope, wq_rope.reshape(MLA_Q_LORA, -1),
                              wq_rope[:, :, swap].reshape(MLA_Q_LORA, -1)], axis=1).astype(BF16)
    wkv = jnp.concatenate([mla_w_uk[0], mla_w_uv[0]], axis=1).astype(BF16)
    tabq, tabk = _rope_tables(t, tc)
    q, k, v = _mla_qkv(hcat, tabq, tabk, mla_q_norm, mla_kv_norm, wq_ext, wkv)
    o_mla = _attention(q, k, v)

    gn = jnp.stack([rwkv_gn_g[0], rwkv_gn_b[0]], axis=0)
    ln1 = jnp.stack([ln1_g[0], ln1_b[0]], axis=0)
    modv = jnp.stack([g1[:b], sh2[:b], sc2[:b]], axis=1)
    x1, u, aff = _mix(alpha, y2, bg2, gg2, o_mla, x, w_out[0].astype(BF16), gn, ln1, modv, router[0].T, bd)

    wt, pos, cnt = _topk(aff, cap)
    ws = cnt[:, :, ::MOE_WIN // LANES].astype(jnp.int32).reshape(-1)
    ye = _moe_ffn(ws, u, pos, exp_w_gate[0].astype(BF16), exp_w_up[0].astype(BF16), exp_w_down[0].astype(BF16), cap)
    ln2 = jnp.stack([ln2_g[0], ln2_b[0]], axis=0)
    return _moe_combine(alpha, ws, ye, jnp.swapaxes(pos, 1, 2), jnp.swapaxes(wt, 1, 2), x1, g2[:b, None, :], ln2)
```

```python
import functools

import jax
import jax.numpy as jnp
import numpy as np
from jax import lax
from jax.experimental import pallas as pl
from jax.experimental.pallas import tpu as pltpu

F32 = jnp.float32
BF16 = jnp.bfloat16

GRID_W = 64
RW_HEADS = 8
RW_HEAD_DIM = 64
RW_WIDTH = RW_HEADS * RW_HEAD_DIM
RW_DECAY_LORA = 64
RW_AAA_LORA = 64
RW_GATE_LORA = 128
MLA_HEADS = 4
MLA_Q_LORA = 256
MLA_KV_LORA = 256
MLA_NOPE_DIM = 128
MLA_ROPE_DIM = 64
MLA_V_DIM = 128
MLA_QK_DIM = MLA_NOPE_DIM + MLA_ROPE_DIM
MLA_V_EXT = 256
ROPE_BASE = 10000.0
N_EXPERTS = 16
CAPACITY_FACTOR = 2
LN_EPS = 1e-5
RMS_EPS = 1e-6
GN_EPS = 64e-5
RW_PROJ = 3 * RW_WIDTH + 2 * RW_DECAY_LORA + 2 * RW_AAA_LORA + RW_GATE_LORA
MLA_PROJ_EXT = MLA_Q_LORA + MLA_KV_LORA + 2 * MLA_ROPE_DIM
IN_PROJ_EXT = RW_PROJ + MLA_PROJ_EXT

ROW_BLK = 256
CHUNK = 64
LANES = 128
HALO = 8
GROUP_LANES = 256
GROUP_HEADS = GROUP_LANES // RW_HEAD_DIM
ATT_TQ = 512
MOE_WIN = 256
SUBLANES_F32 = 8
SUBLANES_BF16 = 16
MOE_GROWS = MOE_WIN + SUBLANES_F32
MOE_CWIN = 128
MOE_CROWS = MOE_CWIN + SUBLANES_BF16
MOE_SEG = 2048
MOE_FIN = 512
MOE_FCHUNK = 256
VMEM_LIMIT = 48 * 1024 * 1024


def _cparams(sem):
    return pltpu.CompilerParams(dimension_semantics=sem, vmem_limit_bytes=VMEM_LIMIT)


def _split2(x):
    hi = x.astype(BF16)
    lo = (x - hi.astype(F32)).astype(BF16)
    return hi, lo


def _split3(x):
    hi = x.astype(BF16)
    r = x - hi.astype(F32)
    mid = r.astype(BF16)
    lo = (r - mid.astype(F32)).astype(BF16)
    return hi, mid, lo


def _mm3(a, b, trans_a=False, trans_b=False):
    ah, al = _split2(a)
    bh, bl = _split2(b)
    d = functools.partial(pl.dot, trans_a=trans_a, trans_b=trans_b)
    return d(ah, bh) + (d(ah, bl) + d(al, bh))


def _mm1(a, b):
    return pl.dot(a.astype(BF16), b.astype(BF16))


def _head_sums(x, group_ones, passes):
    parts = _split2(x)
    out = []
    for gi in range(x.shape[1] // GROUP_LANES):
        ls = slice(gi * GROUP_LANES, (gi + 1) * GROUP_LANES)
        acc = pl.dot(parts[0][:, ls], group_ones)
        if passes == 2:
            acc = acc + pl.dot(parts[1][:, ls], group_ones)
        out.append(acc)
    return jnp.concatenate(out, axis=1)


def _mm_sel_rhs(a, sel):
    ah, al = _split2(a)
    return pl.dot(ah, sel) + pl.dot(al, sel)


def _mm_sel_lhs3(sel, b):
    bh, bm, bl = _split3(b)
    return pl.dot(sel, bh) + (pl.dot(sel, bm) + pl.dot(sel, bl))


def _sigmoid(x):
    return 1.0 / (1.0 + jnp.exp(-x))


def _standardize(x, eps):
    mu = jnp.mean(x, axis=-1, keepdims=True)
    xc = x - mu
    var = jnp.mean(xc * xc, axis=-1, keepdims=True)
    return xc * lax.rsqrt(var + eps)


def _ada_kernel(c_ref, w_ref, b_ref, o_ref):
    c = c_ref[...]
    s = c * _sigmoid(c)
    o_ref[...] = _mm3(s, w_ref[...]) + b_ref[...]


def _ada(cc, w, b):
    rows, d = cc.shape
    n = w.shape[1]
    return pl.pallas_call(
        _ada_kernel,
        out_shape=jax.ShapeDtypeStruct((rows, n), F32),
        grid=(n // d,),
        in_specs=[pl.BlockSpec((rows, d), lambda j: (0, 0)),
                  pl.BlockSpec((d, d), lambda j: (0, j)),
                  pl.BlockSpec((1, d), lambda j: (0, j))],
        out_specs=pl.BlockSpec((rows, d), lambda j: (0, j)),
        compiler_params=_cparams(("arbitrary",)),
        name="ada",
    )(cc, w, b)


def _inproj_kernel(ctx_ref, x_ref, ss_ref, w_ref, o_ref):
    xn = _standardize(jnp.where(pl.program_id(1) == 0, ctx_ref[0], x_ref[0]), LN_EPS)
    shift = ss_ref[0, 0, 0:1, :]
    scale = ss_ref[0, 0, 1:2, :]
    m = (xn * (1.0 + scale) + shift).astype(BF16)
    o_ref[0] = pl.dot(m, w_ref[...])


def _inproj(ctx, x, ss, w_ext):
    b, t, d = x.shape
    tt = t + ctx.shape[1]
    n = w_ext.shape[1]
    return pl.pallas_call(
        _inproj_kernel,
        out_shape=jax.ShapeDtypeStruct((b, tt, n), F32),
        grid=(b, tt // ROW_BLK),
        in_specs=[pl.BlockSpec((1, ROW_BLK, d), lambda bi, i: (bi, 0, 0)),
                  pl.BlockSpec((1, ROW_BLK, d), lambda bi, i: (bi, jnp.maximum(i, 1) - 1, 0)),
                  pl.BlockSpec((1, 1, 2, d), lambda bi, i: (bi, jnp.minimum(i, 1), 0, 0)),
                  pl.BlockSpec((d, n), lambda bi, i: (0, 0))],
        out_specs=pl.BlockSpec((1, ROW_BLK, n), lambda bi, i: (bi, i, 0)),
        compiler_params=_cparams(("arbitrary", "arbitrary")),
        name="inproj",
    )(ctx, x, ss, w_ext)


def _rwkv_ctx_first(i, nblk, fwd):
    return i if fwd else jnp.where(i == 0, 0, nblk - i)


def _bdiag(head_mask, x):
    xb = x.astype(BF16)
    return jnp.concatenate([xb] * GROUP_HEADS, axis=0) * head_mask


def _stack_bf16(top, bottom):
    return jnp.concatenate([top, bottom], axis=0).astype(BF16)


def _rwkv_direction(fwd, head_mask, pb, nblk, hm_ref, hp_ref, hn_ref, conv_ref, w0, wup, a0, aup, gup_ref, vec_ref,
                    bd_ref, bonus_ref, g_ref, rt_ref, kt_ref, at_ref, bt_ref, v_ref, a2_ref, k2_ref, gc_ref,
                    w_ref, ub_ref, mra_ref, yv_ref):
    w3 = 3 * RW_WIDTH
    n_chunks = ROW_BLK // CHUNK

    rkv = hm_ref[0, :, 0:w3]
    first = jnp.logical_or(pb == 0, pb == 1)
    last = jnp.logical_or(pb == 0, pb == nblk - 1)
    prev_row = jnp.where(first, 0.0, hp_ref[0, HALO - 1:HALO, :])
    next_row = jnp.where(last, 0.0, hn_ref[0, 0:1, :])
    rows = lax.broadcasted_iota(jnp.int32, (ROW_BLK, 1), 0)
    up = jnp.where(rows == 0, prev_row, pltpu.roll(rkv, 1, 0))
    dn = jnp.where(rows == ROW_BLK - 1, next_row, pltpu.roll(rkv, ROW_BLK - 1, 0))
    rkv = conv_ref[0:1, :] * up + conv_ref[1:2, :] * rkv + conv_ref[2:3, :] * dn
    r = rkv[:, 0:RW_WIDTH]
    k = rkv[:, RW_WIDTH:2 * RW_WIDTH]
    v = rkv[:, 2 * RW_WIDTH:w3]

    dcol = 0 if fwd else 1
    wdn = hm_ref[0, :, w3 + dcol * RW_DECAY_LORA:w3 + (dcol + 1) * RW_DECAY_LORA]
    a_base = w3 + 2 * RW_DECAY_LORA
    adn = hm_ref[0, :, a_base + dcol * RW_AAA_LORA:a_base + (dcol + 1) * RW_AAA_LORA]

    k_k = vec_ref[0:1, :]
    k_a = vec_ref[1:2, :]
    r_k = vec_ref[2:3, :]
    bd = bd_ref[...]

    z = w0 + _mm1(jnp.tanh(wdn), wup)
    ld = (-float(np.exp(-0.5))) * _sigmoid(z)
    a = _sigmoid(a0 + _mm1(adn, aup))
    kd = k * (1.0 + (a - 1.0) * k_a)
    kkr = k * k_k
    kk = kkr * lax.rsqrt(_head_sums(kkr * kkr, bd, 2) + 1e-12)
    bonus_ref[0] = _head_sums(r * kd * r_k, bd, 1) * v
    if g_ref is not None:
        g_ref[0] = _mm1(_sigmoid(hm_ref[0, :, RW_PROJ - RW_GATE_LORA:RW_PROJ]), gup_ref[...])

    ri = lax.broadcasted_iota(jnp.int32, (ROW_BLK, ROW_BLK), 0)
    ci = lax.broadcasted_iota(jnp.int32, (ROW_BLK, ROW_BLK), 1)
    same = (ri & -CHUNK) == (ci & -CHUNK)
    tri = jnp.logical_and(same, (ri >= ci) if fwd else (ri <= ci))
    cl = _mm_sel_lhs3(jnp.where(tri, 1.0, 0.0).astype(BF16), ld)
    end = CHUNK - 1 if fwd else 0
    tot = jnp.concatenate([jnp.broadcast_to(cl[c * CHUNK + end:c * CHUNK + end + 1], (CHUNK, RW_WIDTH))
                           for c in range(n_chunks)], axis=0)
    e_neg = jnp.exp(-cl)
    e_rem = jnp.exp(tot - cl)
    ka = kk * a
    rt_ref[...] = r * jnp.exp(cl)
    kt_ref[...] = kd * e_neg
    at_ref[...] = -ka * e_neg
    bt_ref[...] = kk * jnp.exp(cl - ld)
    v_ref[...] = v
    a2_ref[...] = -ka * e_rem
    k2_ref[...] = kd * e_rem
    gc_ref[...] = jnp.exp(tot)

    ii = lax.broadcasted_iota(jnp.int32, (CHUNK, GROUP_LANES), 0)
    jj = lax.broadcasted_iota(jnp.int32, (CHUNK, GROUP_LANES), 1) & (CHUNK - 1)
    strict = (ii > jj) if fwd else (ii < jj)
    incl = (ii >= jj) if fwd else (ii <= jj)
    eye = jnp.where(ii == jj, 1.0, 0.0)
    bdiag = functools.partial(_bdiag, head_mask)
    stack = _stack_bf16
    chains = [(slice(c * CHUNK, (c + 1) * CHUNK), slice(gi * GROUP_LANES, (gi + 1) * GROUP_LANES))
              for c in range(n_chunks) for gi in range(RW_WIDTH // GROUP_LANES)]

    pw, tt, lbkv = {}, {}, {}
    for n, key in enumerate(chains):
        lhs = stack(bt_ref[key], rt_ref[key])
        ga = pl.dot(lhs, bdiag(at_ref[key]), trans_b=True)
        gk = pl.dot(lhs, bdiag(kt_ref[key]), trans_b=True)
        mra_ref[key] = jnp.where(incl, ga[CHUNK:], 0.0)
        pw[n] = jnp.where(strict, ga[:CHUNK], 0.0)
        tt[n] = eye + pw[n]
        lm = stack(jnp.where(strict, gk[:CHUNK], 0.0), jnp.where(incl, gk[CHUNK:], 0.0))
        lv = pl.dot(lm, bdiag(v_ref[key]))
        lbkv[n] = lv[:CHUNK]
        yv_ref[key] = lv[CHUNK:]
    for n, key in enumerate(chains):
        pw[n] = pl.dot(pw[n].astype(BF16), bdiag(pw[n]))
    for _ in range(4):
        for n, key in enumerate(chains):
            st = pl.dot(stack(pw[n], tt[n]), bdiag(pw[n]))
            pw[n] = st[:CHUNK]
            tt[n] = tt[n] + st[CHUNK:]
    for n, key in enumerate(chains):
        t_inv = (tt[n] + pl.dot(tt[n].astype(BF16), bdiag(pw[n]))).astype(BF16)
        w_ref[key] = pl.dot(t_inv, bdiag(bt_ref[key]))
        ub_ref[key] = pl.dot(t_inv, bdiag(lbkv[n]))


def _rwkv_state_pass(same_head, head_mask, directions):
    bdiag = functools.partial(_bdiag, head_mask)
    n_chunks = ROW_BLK // CHUNK
    for step in range(n_chunks):
        chains = []
        for fwd, y_ref, s_ref, rt_ref, v_ref, a2_ref, k2_ref, gc_ref, w_ref, ub_ref, mra_ref, yv_ref in directions:
            c = step if fwd else n_chunks - 1 - step
            rs = slice(c * CHUNK, (c + 1) * CHUNK)
            for gi in range(RW_WIDTH // GROUP_LANES):
                ls = slice(gi * GROUP_LANES, (gi + 1) * GROUP_LANES)
                chains.append((y_ref, s_ref, rt_ref, v_ref, a2_ref, k2_ref, gc_ref, w_ref, ub_ref, mra_ref, yv_ref,
                               rs, ls, c))
        s0s, wss, us = [], [], []
        for y_ref, s_ref, rt_ref, v_ref, a2_ref, k2_ref, gc_ref, w_ref, ub_ref, mra_ref, yv_ref, rs, ls, c in chains:
            s0 = s_ref[:, ls]
            s_hi, s_lo = _split2(s0)
            lhs = _stack_bf16(w_ref[rs, ls], rt_ref[rs, ls])
            s0s.append(s0)
            wss.append(pl.dot(lhs, bdiag(s_hi), trans_b=True) + pl.dot(lhs, bdiag(s_lo), trans_b=True))
        for n, (y_ref, s_ref, rt_ref, v_ref, a2_ref, k2_ref, gc_ref, w_ref, ub_ref, mra_ref, yv_ref, rs, ls,
                c) in enumerate(chains):
            u = ub_ref[rs, ls] + wss[n][:CHUNK]
            us.append(u)
            uv = jnp.concatenate([u, v_ref[rs, ls]], axis=0)
            ak2 = jnp.concatenate([a2_ref[rs, ls], k2_ref[rs, ls]], axis=0)
            full = jnp.where(same_head, pl.dot(uv, ak2, trans_a=True), 0.0)
            inc = full[0:RW_HEAD_DIM]
            for hh in range(1, GROUP_HEADS):
                inc = inc + full[hh * RW_HEAD_DIM:(hh + 1) * RW_HEAD_DIM]
            s_ref[:, ls] = s0s[n] * gc_ref[c * CHUNK:c * CHUNK + 1, ls] + inc
        for n, (y_ref, s_ref, rt_ref, v_ref, a2_ref, k2_ref, gc_ref, w_ref, ub_ref, mra_ref, yv_ref, rs, ls,
                c) in enumerate(chains):
            y_ref[0, rs, ls] = (wss[n][CHUNK:] + yv_ref[rs, ls]
                                + pl.dot(mra_ref[rs, ls].astype(BF16), bdiag(us[n])))


RWKV_DIR_SCRATCH = 13


def _rwkv_kernel(hmf_ref, hpf_ref, hnf_ref, hmb_ref, hpb_ref, hnb_ref, conv_ref, w0_ref, wup_ref, a0_ref, aup_ref,
                 gup_ref, vec_ref, bd_ref, yf_ref, yb_ref, bonf_ref, bonb_ref, g_ref, *scratch):
    i = pl.program_id(1)
    nblk = pl.num_programs(1)
    sf, sb = scratch[:RWKV_DIR_SCRATCH], scratch[RWKV_DIR_SCRATCH:]

    @pl.when(i == 0)
    def _():
        sf[0][...] = jnp.zeros_like(sf[0])
        sb[0][...] = jnp.zeros_like(sb[0])

    r4 = lax.broadcasted_iota(jnp.int32, (GROUP_LANES, GROUP_LANES), 0) & -RW_HEAD_DIM
    c4 = lax.broadcasted_iota(jnp.int32, (GROUP_LANES, GROUP_LANES), 1) & -RW_HEAD_DIM
    same_head = r4 == c4
    head_mask = jnp.where(same_head, 1.0, 0.0).astype(BF16)
    _rwkv_direction(True, head_mask, _rwkv_ctx_first(i, nblk, True), nblk, hmf_ref, hpf_ref, hnf_ref, conv_ref,
                    w0_ref[0], wup_ref[0], a0_ref[0], aup_ref[0], gup_ref, vec_ref, bd_ref, bonf_ref, g_ref, *sf[1:])
    _rwkv_direction(False, head_mask, _rwkv_ctx_first(i, nblk, False), nblk, hmb_ref, hpb_ref, hnb_ref, conv_ref,
                    w0_ref[1], wup_ref[1], a0_ref[1], aup_ref[1], gup_ref, vec_ref, bd_ref, bonb_ref, None, *sb[1:])

    def state_refs(fwd, y_ref, scr):
        s_ref, rt_ref, _, _, _, v_ref, a2_ref, k2_ref, gc_ref, w_ref, ub_ref, mra_ref, yv_ref = scr
        return fwd, y_ref, s_ref, rt_ref, v_ref, a2_ref, k2_ref, gc_ref, w_ref, ub_ref, mra_ref, yv_ref

    _rwkv_state_pass(same_head, head_mask, [state_refs(True, yf_ref, sf), state_refs(False, yb_ref, sb)])


def _rwkv(hcat, conv, w0, wup, a0, aup, gup, vecs, bd):
    b, tt, _ = hcat.shape
    nblk = tt // ROW_BLK
    t = tt - ROW_BLK
    w3 = 3 * RW_WIDTH
    hb = ROW_BLK // HALO

    def in_specs(fwd):
        pbf = lambda i: _rwkv_ctx_first(i, nblk, fwd)
        return [pl.BlockSpec((1, ROW_BLK, RW_PROJ), lambda bi, i: (bi, pbf(i), 0)),
                pl.BlockSpec((1, HALO, w3), lambda bi, i: (bi, jnp.maximum(pbf(i) * hb - 1, 0), 0)),
                pl.BlockSpec((1, HALO, w3), lambda bi, i: (bi, jnp.minimum((pbf(i) + 1) * hb, nblk * hb - 1), 0))]

    def out_spec(fwd):
        return pl.BlockSpec((1, ROW_BLK, RW_WIDTH),
                            lambda bi, i: (bi, _rwkv_ctx_first(jnp.maximum(i, 1), nblk, fwd) - 1, 0))

    full = lambda a: pl.BlockSpec(a.shape, lambda bi, i: (0,) * a.ndim)
    scr = [pltpu.VMEM((RW_HEAD_DIM, RW_WIDTH), F32)] + [pltpu.VMEM((ROW_BLK, RW_WIDTH), F32)] * (RWKV_DIR_SCRATCH - 1)
    out = jax.ShapeDtypeStruct((b, t, RW_WIDTH), F32)
    return pl.pallas_call(
        _rwkv_kernel,
        out_shape=[out] * 5,
        grid=(b, nblk),
        in_specs=in_specs(True) + in_specs(False) + [full(a) for a in (conv, w0, wup, a0, aup, gup, vecs, bd)],
        out_specs=[out_spec(True), out_spec(False), out_spec(True), out_spec(False), out_spec(True)],
        scratch_shapes=scr + scr,
        compiler_params=_cparams(("arbitrary", "arbitrary")),
        name="rwkv",
    )(hcat, hcat, hcat, hcat, hcat, hcat, conv, w0, wup, a0, aup, gup, vecs, bd)


def _mla_kernel(h_ref, tq_ref, tk_ref, qn_ref, kvn_ref, wq_ref, wkv_ref, q_ref, k_ref, v_ref):
    hq = h_ref[0, :, 0:MLA_Q_LORA]
    hkv = h_ref[0, :, MLA_Q_LORA:MLA_Q_LORA + MLA_KV_LORA]
    kr = h_ref[0, :, MLA_Q_LORA + MLA_KV_LORA:MLA_Q_LORA + MLA_KV_LORA + MLA_ROPE_DIM]
    krs = h_ref[0, :, MLA_Q_LORA + MLA_KV_LORA + MLA_ROPE_DIM:MLA_PROJ_EXT]

    def rms(x, gain):
        return x * lax.rsqrt(jnp.mean(x * x, axis=-1, keepdims=True) + RMS_EPS) * gain

    qall = pl.dot(rms(hq, qn_ref[...]).astype(BF16), wq_ref[...])
    kvall = pl.dot(rms(hkv, kvn_ref[...]).astype(BF16), wkv_ref[...])
    nn = MLA_HEADS * MLA_NOPE_DIM
    nr = MLA_HEADS * MLA_ROPE_DIM
    qrot = qall[:, nn:nn + nr] * tq_ref[:, 0:nr] + qall[:, nn + nr:nn + 2 * nr] * tq_ref[:, nr:2 * nr]
    krot = kr * tk_ref[:, 0:MLA_ROPE_DIM] + krs * tk_ref[:, MLA_ROPE_DIM:2 * MLA_ROPE_DIM]
    scale = MLA_QK_DIM ** -0.5
    ones_col = jnp.where(lax.broadcasted_iota(jnp.int32, (ROW_BLK, MLA_V_EXT - MLA_V_DIM), 1) == 0, 1.0, 0.0)
    for h in range(MLA_HEADS):
        qh = jnp.concatenate([qall[:, h * MLA_NOPE_DIM:(h + 1) * MLA_NOPE_DIM],
                              qrot[:, h * MLA_ROPE_DIM:(h + 1) * MLA_ROPE_DIM]], axis=1)
        q_ref[0, h] = (qh * scale).astype(BF16)
        kh = jnp.concatenate([kvall[:, h * MLA_NOPE_DIM:(h + 1) * MLA_NOPE_DIM], krot], axis=1)
        k_ref[0, h] = kh.astype(BF16)
        vh = kvall[:, nn + h * MLA_V_DIM:nn + (h + 1) * MLA_V_DIM]
        v_ref[0, h] = jnp.concatenate([vh, ones_col], axis=1).astype(BF16)


def _mla_qkv(hcat, tabq, tabk, qn, kvn, wq, wkv):
    b, tt, _ = hcat.shape
    t = tt - ROW_BLK
    lat = lambda i: jnp.maximum(i, 1) - 1
    return pl.pallas_call(
        _mla_kernel,
        out_shape=[jax.ShapeDtypeStruct((b, MLA_HEADS, t, MLA_QK_DIM), BF16),
                   jax.ShapeDtypeStruct((b, MLA_HEADS, tt, MLA_QK_DIM), BF16),
                   jax.ShapeDtypeStruct((b, MLA_HEADS, tt, MLA_V_EXT), BF16)],
        grid=(b, tt // ROW_BLK),
        in_specs=[pl.BlockSpec((1, ROW_BLK, MLA_PROJ_EXT), lambda bi, i: (bi, i, RW_PROJ // MLA_PROJ_EXT)),
                  pl.BlockSpec((ROW_BLK, 2 * MLA_HEADS * MLA_ROPE_DIM), lambda bi, i: (lat(i), 0)),
                  pl.BlockSpec((ROW_BLK, 2 * MLA_ROPE_DIM), lambda bi, i: (i, 0)),
                  pl.BlockSpec((1, MLA_Q_LORA), lambda bi, i: (0, 0)),
                  pl.BlockSpec((1, MLA_KV_LORA), lambda bi, i: (0, 0)),
                  pl.BlockSpec(wq.shape, lambda bi, i: (0, 0)),
                  pl.BlockSpec(wkv.shape, lambda bi, i: (0, 0))],
        out_specs=[pl.BlockSpec((1, MLA_HEADS, ROW_BLK, MLA_QK_DIM), lambda bi, i: (bi, 0, lat(i), 0)),
                   pl.BlockSpec((1, MLA_HEADS, ROW_BLK, MLA_QK_DIM), lambda bi, i: (bi, 0, i, 0)),
                   pl.BlockSpec((1, MLA_HEADS, ROW_BLK, MLA_V_EXT), lambda bi, i: (bi, 0, i, 0))],
        compiler_params=_cparams(("arbitrary", "arbitrary")),
        name="mla_qkv",
    )(hcat, tabq, tabk, qn, kvn, wq, wkv)


def _attn_kernel(tk, q_ref, k_ref, v_ref, o_ref):
    q = q_ref[0, 0]
    tq = q.shape[0]
    m = jnp.full((tq, 1), -jnp.inf, F32)
    acc = jnp.zeros((tq, v_ref.shape[3]), F32)
    for j in range(k_ref.shape[2] // tk):
        s = pl.dot(q, k_ref[0, 0, j * tk:(j + 1) * tk, :], trans_b=True)
        m_new = jnp.maximum(m, jnp.max(s, axis=-1, keepdims=True))
        p = jnp.exp(s - m_new).astype(BF16)
        acc = jnp.exp(m - m_new) * acc + pl.dot(p, v_ref[0, 0, j * tk:(j + 1) * tk, :])
        m = m_new
    o_ref[0] = acc[:, :MLA_V_DIM] / acc[:, MLA_V_DIM:MLA_V_DIM + 1]


def _attn_tk(tk_total):
    for cand in (768, 1024, 512, 384, 256):
        if tk_total % cand == 0:
            return cand
    return ROW_BLK


def _attention(q, k, v):
    b, nh, t, dq = q.shape
    tk_total, dv = k.shape[2], v.shape[3]
    tq = min(ATT_TQ, t)
    return pl.pallas_call(
        functools.partial(_attn_kernel, _attn_tk(tk_total)),
        out_shape=jax.ShapeDtypeStruct((b, t, nh * MLA_V_DIM), F32),
        grid=(b, nh, t // tq),
        in_specs=[pl.BlockSpec((1, 1, tq, dq), lambda bi, h, qi: (bi, h, qi, 0)),
                  pl.BlockSpec((1, 1, tk_total, dq), lambda bi, h, qi: (bi, h, 0, 0)),
                  pl.BlockSpec((1, 1, tk_total, dv), lambda bi, h, qi: (bi, h, 0, 0))],
        out_specs=pl.BlockSpec((1, tq, MLA_V_DIM), lambda bi, h, qi: (bi, qi, h)),
        compiler_params=_cparams(("arbitrary", "arbitrary", "arbitrary")),
        name="attn",
    )(q, k, v)


def _mix_kernel(alpha, yf_ref, yb_ref, bonf_ref, bonb_ref, g_ref, o_ref, x_ref, wout_ref, gn_ref, ln_ref, mod_ref,
                rt_ref, bd_ref, x1_ref, u_ref, aff_ref):
    bd = bd_ref[...]
    inv_n = 1.0 / RW_HEAD_DIM
    ysum = yf_ref[0] + yb_ref[0]
    yc = ysum - _mm_sel_rhs(ysum, bd) * inv_n
    var = _mm_sel_rhs(yc * yc, bd) * inv_n
    yn = yc * lax.rsqrt(var + GN_EPS) * gn_ref[0:1, :] + gn_ref[1:2, :]
    rw = (yn + (bonf_ref[0] + bonb_ref[0])) * g_ref[0]
    mix = (pl.dot(rw.astype(BF16), wout_ref[0:RW_WIDTH, :])
           + pl.dot(o_ref[0].astype(BF16), wout_ref[RW_WIDTH:, :]))
    g1 = mod_ref[0, 0:1, :]
    sh2 = mod_ref[0, 1:2, :]
    sc2 = mod_ref[0, 2:3, :]
    x1 = _standardize(alpha * x_ref[0] + g1 * mix, LN_EPS) * ln_ref[0:1, :] + ln_ref[1:2, :]
    x1_ref[0] = x1
    u = _standardize(x1, LN_EPS) * (1.0 + sc2) + sh2
    u_ref[0] = u.astype(BF16)
    logits = _mm3(rt_ref[...], u, trans_b=True)
    ex = jnp.exp(logits - jnp.max(logits, axis=0, keepdims=True))
    aff_ref[0] = ex / jnp.sum(ex, axis=0, keepdims=True)


def _mix(alpha, yf, yb, bonf, bonb, g, o_mla, x, wout, gn, ln, modv, router_t, bd):
    b, t, d = x.shape
    ne = router_t.shape[0]
    row = lambda w: pl.BlockSpec((1, ROW_BLK, w), lambda bi, i: (bi, i, 0))
    full = lambda a: pl.BlockSpec(a.shape, lambda bi, i: (0,) * a.ndim)
    return pl.pallas_call(
        functools.partial(_mix_kernel, alpha),
        out_shape=[jax.ShapeDtypeStruct((b, t, d), F32),
                   jax.ShapeDtypeStruct((b, t, d), BF16),
                   jax.ShapeDtypeStruct((b, ne, t), F32)],
        grid=(b, t // ROW_BLK),
        in_specs=[row(RW_WIDTH)] * 5 + [
            row(MLA_HEADS * MLA_V_DIM), row(d), full(wout), full(gn), full(ln),
            pl.BlockSpec((1, 3, d), lambda bi, i: (bi, 0, 0)), full(router_t), full(bd)],
        out_specs=[row(d), row(d), pl.BlockSpec((1, ne, ROW_BLK), lambda bi, i: (bi, 0, i))],
        compiler_params=_cparams(("arbitrary", "arbitrary")),
        name="mix",
    )(yf, yb, bonf, bonb, g, o_mla, x, wout, gn, ln, modv, router_t, bd)


def _topk_kernel(cap, aff_ref, w_ref, pos_ref, cnt_ref):
    aff = aff_ref[0]
    ne, t = aff.shape
    bits = lax.bitcast_convert_type(aff, jnp.int32)

    def count_ge(thr):
        return jnp.sum(jnp.where(bits >= thr, 1.0, 0.0), axis=1, keepdims=True)

    def body(it, thr):
        cand = thr | lax.shift_left(jnp.int32(1), 30 - it)
        return jnp.where(count_ge(cand) >= cap, cand, thr)

    thr = lax.fori_loop(0, 31, body, jnp.zeros((ne, 1), jnp.int32))
    need = cap - jnp.sum(jnp.where(bits > thr, 1.0, 0.0), axis=1, keepdims=True)
    ri = lax.broadcasted_iota(jnp.int32, (LANES, LANES), 0)
    ci = lax.broadcasted_iota(jnp.int32, (LANES, LANES), 1)
    below = jnp.where(ri < ci, 1.0, 0.0).astype(BF16)
    ties = jnp.zeros((ne, 1), F32)
    taken = jnp.zeros((ne, 1), F32)
    for j in range(t // LANES):
        sl = slice(j * LANES, (j + 1) * LANES)
        bj = bits[:, sl]
        eqj = bj == thr
        eqf = jnp.where(eqj, 1.0, 0.0)
        before = pl.dot(eqf.astype(BF16), below) + ties
        take = jnp.logical_or(bj > thr, jnp.logical_and(eqj, before < need))
        takef = jnp.where(take, 1.0, 0.0)
        w_ref[0, :, sl] = jnp.where(take, aff[:, sl], 0.0)
        pos_ref[0, :, sl] = jnp.where(take, pl.dot(takef.astype(BF16), below) + taken, -1.0)
        cnt_ref[0, :, j:j + 1] = taken
        ties = ties + jnp.sum(eqf, axis=1, keepdims=True)
        taken = taken + jnp.sum(takef, axis=1, keepdims=True)


def _topk(aff, cap):
    b, ne, t = aff.shape
    spec = pl.BlockSpec((1, ne, t), lambda bi: (bi, 0, 0))
    return pl.pallas_call(
        functools.partial(_topk_kernel, float(cap)),
        out_shape=[jax.ShapeDtypeStruct((b, ne, t), F32), jax.ShapeDtypeStruct((b, ne, t), F32),
                   jax.ShapeDtypeStruct((b, ne, t // LANES), F32)],
        grid=(b,),
        in_specs=[spec],
        out_specs=[spec, spec, pl.BlockSpec((1, ne, t // LANES), lambda bi: (bi, 0, 0))],
        compiler_params=_cparams(("arbitrary",)),
        name="topk",
    )(aff)


def _moe_ffn_kernel(cap, ws_ref, u_ref, pos_ref, wg_ref, wu_ref, wd_ref, ye_ref, xe_ref, acc_ref):
    bi, e, q = pl.program_id(0), pl.program_id(1), pl.program_id(2)
    ne, nq = pl.num_programs(1), pl.num_programs(2)
    nwin = u_ref.shape[1] // MOE_WIN
    f = wg_ref.shape[2]

    @pl.when(q == 0)
    def _():
        xe_ref[...] = jnp.zeros_like(xe_ref)

    rows = lax.broadcasted_iota(jnp.int32, (MOE_GROWS, 1), 0).astype(F32)
    for wl in range(nwin):
        start = pl.multiple_of(ws_ref[(bi * ne + e) * (nq * nwin) + q * nwin + wl] & -SUBLANES_F32, SUBLANES_F32)
        rank = pos_ref[0, pl.ds(e, 1), wl * MOE_WIN:(wl + 1) * MOE_WIN] - start.astype(F32)
        onehot = jnp.where(rank == rows, 1.0, 0.0).astype(BF16)
        xe_ref[pl.ds(start, MOE_GROWS), :] += pl.dot(onehot, u_ref[0, wl * MOE_WIN:(wl + 1) * MOE_WIN, :])

    @pl.when(q == nq - 1)
    def _():
        xb = xe_ref[0:cap, :].astype(BF16)
        for c in range(f // MOE_FCHUNK):
            cs = slice(c * MOE_FCHUNK, (c + 1) * MOE_FCHUNK)
            hg = pl.dot(xb, wg_ref[0, :, cs])
            hu = pl.dot(xb, wu_ref[0, :, cs])
            part = pl.dot((hg * _sigmoid(hg) * hu).astype(BF16), wd_ref[0, cs, :])
            if c == 0:
                acc_ref[...] = part
            else:
                acc_ref[...] += part
        ye_ref[0, 0, 0:cap, :] = acc_ref[...].astype(BF16)
        ye_ref[0, 0, cap:, :] = jnp.zeros((ye_ref.shape[2] - cap, ye_ref.shape[3]), BF16)


def _moe_ffn(ws, u, pos, wg, wu, wd, cap):
    b, t, d = u.shape
    ne, _, f = wg.shape
    seg = min(MOE_SEG, t)
    cap_pad = cap + MOE_CROWS
    grid_spec = pltpu.PrefetchScalarGridSpec(
        num_scalar_prefetch=1,
        grid=(b, ne, t // seg),
        in_specs=[pl.BlockSpec((1, seg, d), lambda bi, e, q, ws: (bi, q, 0)),
                  pl.BlockSpec((1, ne, seg), lambda bi, e, q, ws: (bi, 0, q)),
                  pl.BlockSpec((1, d, f), lambda bi, e, q, ws: (e, 0, 0)),
                  pl.BlockSpec((1, d, f), lambda bi, e, q, ws: (e, 0, 0)),
                  pl.BlockSpec((1, f, d), lambda bi, e, q, ws: (e, 0, 0))],
        out_specs=pl.BlockSpec((1, 1, cap_pad, d), lambda bi, e, q, ws: (bi, e, 0, 0)),
        scratch_shapes=[pltpu.VMEM((cap + MOE_GROWS, d), F32), pltpu.VMEM((cap, d), F32)])
    return pl.pallas_call(
        functools.partial(_moe_ffn_kernel, cap),
        out_shape=jax.ShapeDtypeStruct((b, ne, cap_pad, d), BF16),
        grid_spec=grid_spec,
        compiler_params=_cparams(("arbitrary", "arbitrary", "arbitrary")),
        name="moe_ffn",
    )(ws, u, pos, wg, wu, wd)


def _moe_combine_kernel(alpha, ne, ws_ref, ye_ref, pos_ref, wt_ref, x1_ref, g2_ref, ln_ref, o_ref, acc_ref):
    bi, qq, s = pl.program_id(0), pl.program_id(1), pl.program_id(2)
    nwin = acc_ref.shape[0] // MOE_CWIN
    total_win = pl.num_programs(1) * nwin

    @pl.when(s == 0)
    def _():
        acc_ref[...] = jnp.zeros_like(acc_ref)

    @pl.when(s < ne)
    def _():
        lane = lax.broadcasted_iota(jnp.int32, (MOE_CWIN, ne), 1)
        cols = lax.broadcasted_iota(jnp.int32, (1, MOE_CROWS), 1).astype(F32)
        for wl in range(nwin):
            ts = slice(wl * MOE_CWIN, (wl + 1) * MOE_CWIN)
            start = pl.multiple_of(ws_ref[(bi * ne + s) * total_win + qq * nwin + wl] & -SUBLANES_BF16, SUBLANES_BF16)
            rank = jnp.sum(jnp.where(lane == s, pos_ref[0, ts, :], 0.0), axis=1, keepdims=True) - start.astype(F32)
            gate = jnp.sum(jnp.where(lane == s, wt_ref[0, ts, :], 0.0), axis=1, keepdims=True)
            onehot = jnp.where(rank == cols, 1.0, 0.0).astype(BF16)
            acc_ref[ts, :] += gate * pl.dot(onehot, ye_ref[0, 0, pl.ds(start, MOE_CROWS), :])

    @pl.when(s >= ne)
    def _():
        r0 = pl.multiple_of((s - ne) * o_ref.shape[1], o_ref.shape[1])
        xr = alpha * x1_ref[0] + g2_ref[0] * acc_ref[pl.ds(r0, o_ref.shape[1]), :]
        o_ref[0] = _standardize(xr, LN_EPS) * ln_ref[0:1, :] + ln_ref[1:2, :]


def _moe_combine(alpha, ws, ye, pos_t, wt_t, x1, g2, ln):
    b, t, d = x1.shape
    ne, cap_pad = ye.shape[1], ye.shape[2]
    seg = min(MOE_SEG, t)
    fin = min(MOE_FIN, seg)
    nfin = seg // fin
    fblk = lambda bi, qq, s, ws: (bi, qq * nfin + jnp.maximum(s - ne, 0), 0)
    grid_spec = pltpu.PrefetchScalarGridSpec(
        num_scalar_prefetch=1,
        grid=(b, t // seg, ne + nfin),
        in_specs=[pl.BlockSpec((1, 1, cap_pad, d), lambda bi, qq, s, ws: (bi, jnp.minimum(s, ne - 1), 0, 0)),
                  pl.BlockSpec((1, seg, ne), lambda bi, qq, s, ws: (bi, qq, 0)),
                  pl.BlockSpec((1, seg, ne), lambda bi, qq, s, ws: (bi, qq, 0)),
                  pl.BlockSpec((1, fin, d), fblk),
                  pl.BlockSpec((1, 1, d), lambda bi, qq, s, ws: (bi, 0, 0)),
                  pl.BlockSpec((2, d), lambda bi, qq, s, ws: (0, 0))],
        out_specs=pl.BlockSpec((1, fin, d), fblk),
        scratch_shapes=[pltpu.VMEM((seg, d), F32)])
    return pl.pallas_call(
        functools.partial(_moe_combine_kernel, alpha, ne),
        out_shape=jax.ShapeDtypeStruct((b, t, d), F32),
        grid_spec=grid_spec,
        compiler_params=_cparams(("arbitrary", "arbitrary", "arbitrary")),
        name="moe_combine",
    )(ws, ye, pos_t, wt_t, x1, g2, ln)


def _rope_tables(t, tc):
    half = MLA_ROPE_DIM // 2
    inv_freq = ROPE_BASE ** (-jnp.arange(0, half, 2, dtype=F32) / half)
    pos = jnp.arange(t)
    ang_r = (pos // GRID_W).reshape(-1, 1).astype(F32) * inv_freq
    ang_c = (pos % GRID_W).reshape(-1, 1).astype(F32) * inv_freq
    cos = jnp.concatenate([jnp.cos(ang_r)] * 2 + [jnp.cos(ang_c)] * 2, axis=-1)
    sin = jnp.concatenate([-jnp.sin(ang_r), jnp.sin(ang_r), -jnp.sin(ang_c), jnp.sin(ang_c)], axis=-1)
    tabq = jnp.concatenate([jnp.tile(cos, (1, MLA_HEADS)), jnp.tile(sin, (1, MLA_HEADS))], axis=-1)
    cos_k = jnp.concatenate([jnp.ones((tc, MLA_ROPE_DIM), F32), cos], axis=0)
    sin_k = jnp.concatenate([jnp.zeros((tc, MLA_ROPE_DIM), F32), sin], axis=0)
    return tabq, jnp.concatenate([cos_k, sin_k], axis=-1)


def _pair_swap():
    q = MLA_ROPE_DIM // 4
    return np.concatenate([np.arange(q, 2 * q), np.arange(0, q), np.arange(3 * q, 4 * q), np.arange(2 * q, 3 * q)])


def kernel(x, c, ctx, c_ctx, w_ada, b_ada, w_in, rwkv_conv, rwkv_w0, rwkv_w_up, rwkv_a0, rwkv_a_up, rwkv_g_up, rwkv_k_k, rwkv_k_a, rwkv_r_k, rwkv_gn_g, rwkv_gn_b, mla_q_norm, mla_w_uq, mla_kv_norm, mla_w_uk, mla_w_uv, w_out, ln1_g, ln1_b, router, exp_w_gate, exp_w_up, exp_w_down, ln2_g, ln2_b):
    b, t, d = x.shape
    tc = ctx.shape[1]
    depth = w_ada.shape[0]
    assert depth == 1 and tc == ROW_BLK and t % ROW_BLK == 0 and t % LANES == 0
    alpha = (2.0 * depth) ** 0.25
    cap = CAPACITY_FACTOR * t // N_EXPERTS

    pad = (-(b + 1)) % 8
    cc = jnp.concatenate([c, c_ctx[None, :], jnp.zeros((pad, d), F32)], axis=0)
    mod = _ada(cc, w_ada[0], b_ada)
    sh1, sc1, g1, sh2, sc2, g2 = jnp.split(mod, 6, axis=-1)
    ss_lat = jnp.stack([sh1[:b], sc1[:b]], axis=1)
    ss_ctx = jnp.broadcast_to(jnp.stack([sh1[b], sc1[b]], axis=0)[None], (b, 2, d))
    ss = jnp.stack([ss_ctx, ss_lat], axis=1)

    swap = _pair_swap()
    kr0 = RW_PROJ + MLA_Q_LORA + MLA_KV_LORA
    w_ext = jnp.concatenate([w_in[0], w_in[0][:, kr0 + swap]], axis=1).astype(BF16)
    hcat = _inproj(ctx, x, ss, w_ext)

    head_ones = np.kron(np.eye(RW_HEADS, dtype=np.float32), np.ones((RW_HEAD_DIM, RW_HEAD_DIM), np.float32))
    bd = jnp.asarray(head_ones, BF16)
    vecs = jnp.stack([rwkv_k_k[0], rwkv_k_a[0], rwkv_r_k[0].reshape(-1)], axis=0)
    yf, yb, bonf, bonb, gate = _rwkv(hcat, rwkv_conv[0], rwkv_w0[0][:, None, :], rwkv_w_up[0],
                                     rwkv_a0[0][:, None, :], rwkv_a_up[0], rwkv_g_up[0], vecs,
                                     bd[:GROUP_LANES, :GROUP_LANES])

    wq = mla_w_uq[0].reshape(MLA_Q_LORA, MLA_HEADS, MLA_QK_DIM)
    wq_nope = wq[:, :, :MLA_NOPE_DIM].reshape(MLA_Q_LORA, -1)
    wq_rope = wq[:, :, MLA_NOPE_DIM:]
    wq_ext = jnp.concatenate([wq_nope, wq_rope.reshape(MLA_Q_LORA, -1),
                              wq_rope[:, :, swap].reshape(MLA_Q_LORA, -1)], axis=1).astype(BF16)
    wkv = jnp.concatenate([mla_w_uk[0], mla_w_uv[0]], axis=1).astype(BF16)
    tabq, tabk = _rope_tables(t, tc)
    q, k, v = _mla_qkv(hcat, tabq, tabk, mla_q_norm, mla_kv_norm, wq_ext, wkv)
    o_mla = _attention(q, k, v)

    gn = jnp.stack([rwkv_gn_g[0], rwkv_gn_b[0]], axis=0)
    ln1 = jnp.stack([ln1_g[0], ln1_b[0]], axis=0)
    modv = jnp.stack([g1[:b], sh2[:b], sc2[:b]], axis=1)
    x1, u, aff = _mix(alpha, yf, yb, bonf, bonb, gate, o_mla, x, w_out[0].astype(BF16), gn, ln1, modv, router[0].T, bd)

    wt, pos, cnt = _topk(aff, cap)
    cnt = cnt.astype(jnp.int32)
    ws = cnt[:, :, ::MOE_WIN // LANES].reshape(-1)
    ws_c = cnt[:, :, ::MOE_CWIN // LANES].reshape(-1)
    ye = _moe_ffn(ws, u, pos, exp_w_gate[0].astype(BF16), exp_w_up[0].astype(BF16), exp_w_down[0].astype(BF16), cap)
    ln2 = jnp.stack([ln2_g[0], ln2_b[0]], axis=0)
    return _moe_combine(alpha, ws_c, ye, jnp.swapaxes(pos, 1, 2), jnp.swapaxes(wt, 1, 2), x1, g2[:b, None, :], ln2)
```

```python
import functools

import jax
import jax.numpy as jnp
import numpy as np
from jax import lax
from jax.experimental import pallas as pl
from jax.experimental.pallas import tpu as pltpu

F32 = jnp.float32
BF16 = jnp.bfloat16

GRID_W = 64
RW_HEADS = 8
RW_HEAD_DIM = 64
RW_WIDTH = RW_HEADS * RW_HEAD_DIM
RW_DECAY_LORA = 64
RW_AAA_LORA = 64
RW_GATE_LORA = 128
MLA_HEADS = 4
MLA_Q_LORA = 256
MLA_KV_LORA = 256
MLA_NOPE_DIM = 128
MLA_ROPE_DIM = 64
MLA_V_DIM = 128
MLA_QK_DIM = MLA_NOPE_DIM + MLA_ROPE_DIM
MLA_V_EXT = 256
ROPE_BASE = 10000.0
N_EXPERTS = 16
CAPACITY_FACTOR = 2
LN_EPS = 1e-5
RMS_EPS = 1e-6
GN_EPS = 64e-5
RW_PROJ = 3 * RW_WIDTH + 2 * RW_DECAY_LORA + 2 * RW_AAA_LORA + RW_GATE_LORA
MLA_PROJ_EXT = MLA_Q_LORA + MLA_KV_LORA + 2 * MLA_ROPE_DIM
IN_PROJ_EXT = RW_PROJ + MLA_PROJ_EXT

ROW_BLK = 256
CHUNK = 64
LANES = 128
HALO = 8
GROUP_LANES = 256
GROUP_HEADS = GROUP_LANES // RW_HEAD_DIM
ATT_TQ = 512
MOE_WIN = 256
SUBLANES_F32 = 8
SUBLANES_BF16 = 16
MOE_GROWS = MOE_WIN + SUBLANES_F32
MOE_CWIN = 128
MOE_CROWS = MOE_CWIN + SUBLANES_BF16
MOE_SEG = 2048
MOE_FIN = 512
MOE_FCHUNK = 256
VMEM_LIMIT = 48 * 1024 * 1024


def _cparams(sem):
    return pltpu.CompilerParams(dimension_semantics=sem, vmem_limit_bytes=VMEM_LIMIT)


def _split2(x):
    hi = x.astype(BF16)
    lo = (x - hi.astype(F32)).astype(BF16)
    return hi, lo


def _split3(x):
    hi = x.astype(BF16)
    r = x - hi.astype(F32)
    mid = r.astype(BF16)
    lo = (r - mid.astype(F32)).astype(BF16)
    return hi, mid, lo


def _mm3(a, b, trans_a=False, trans_b=False):
    ah, al = _split2(a)
    bh, bl = _split2(b)
    d = functools.partial(pl.dot, trans_a=trans_a, trans_b=trans_b)
    return d(ah, bh) + (d(ah, bl) + d(al, bh))


def _bdot(lane_masks, a, x, trans_b=False):
    ah, al = _split2(a)
    xh, xl = _split2(x)
    bh = _bdiag(lane_masks, xh)
    d = functools.partial(pl.dot, trans_b=trans_b)
    return d(ah, bh) + (d(ah, _bdiag(lane_masks, xl)) + d(al, bh))


def _head_sums(x, group_ones, passes):
    parts = _split2(x)
    out = []
    for gi in range(x.shape[1] // GROUP_LANES):
        ls = slice(gi * GROUP_LANES, (gi + 1) * GROUP_LANES)
        acc = pl.dot(parts[0][:, ls], group_ones)
        if passes == 2:
            acc = acc + pl.dot(parts[1][:, ls], group_ones)
        out.append(acc)
    return jnp.concatenate(out, axis=1)


def _mm_sel_lhs3(sel, b):
    bh, bm, bl = _split3(b)
    return pl.dot(sel, bh) + (pl.dot(sel, bm) + pl.dot(sel, bl))


def _sigmoid(x):
    return 1.0 / (1.0 + jnp.exp(-x))


def _standardize(x, eps):
    mu = jnp.mean(x, axis=-1, keepdims=True)
    xc = x - mu
    var = jnp.mean(xc * xc, axis=-1, keepdims=True)
    return xc * lax.rsqrt(var + eps)


def _ada_kernel(c_ref, w_ref, b_ref, o_ref):
    c = c_ref[...]
    s = c * _sigmoid(c)
    o_ref[...] = _mm3(s, w_ref[...]) + b_ref[...]


def _ada(cc, w, b):
    rows, d = cc.shape
    n = w.shape[1]
    return pl.pallas_call(
        _ada_kernel,
        out_shape=jax.ShapeDtypeStruct((rows, n), F32),
        grid=(n // d,),
        in_specs=[pl.BlockSpec((rows, d), lambda j: (0, 0)),
                  pl.BlockSpec((d, d), lambda j: (0, j)),
                  pl.BlockSpec((1, d), lambda j: (0, j))],
        out_specs=pl.BlockSpec((rows, d), lambda j: (0, j)),
        compiler_params=_cparams(("arbitrary",)),
        name="ada",
    )(cc, w, b)


def _inproj_kernel(ctx_ref, x_ref, ss_ref, w_ref, o_ref):
    xn = _standardize(jnp.where(pl.program_id(1) == 0, ctx_ref[0], x_ref[0]), LN_EPS)
    shift = ss_ref[0, 0, 0:1, :]
    scale = ss_ref[0, 0, 1:2, :]
    o_ref[0] = _mm3(xn * (1.0 + scale) + shift, w_ref[...])


def _inproj(ctx, x, ss, w_ext):
    b, t, d = x.shape
    tt = t + ctx.shape[1]
    n = w_ext.shape[1]
    return pl.pallas_call(
        _inproj_kernel,
        out_shape=jax.ShapeDtypeStruct((b, tt, n), F32),
        grid=(b, tt // ROW_BLK),
        in_specs=[pl.BlockSpec((1, ROW_BLK, d), lambda bi, i: (bi, 0, 0)),
                  pl.BlockSpec((1, ROW_BLK, d), lambda bi, i: (bi, jnp.maximum(i, 1) - 1, 0)),
                  pl.BlockSpec((1, 1, 2, d), lambda bi, i: (bi, jnp.minimum(i, 1), 0, 0)),
                  pl.BlockSpec((d, n), lambda bi, i: (0, 0))],
        out_specs=pl.BlockSpec((1, ROW_BLK, n), lambda bi, i: (bi, i, 0)),
        compiler_params=_cparams(("arbitrary", "arbitrary")),
        name="inproj",
    )(ctx, x, ss, w_ext)


def _rwkv_ctx_first(i, nblk, fwd):
    return i if fwd else jnp.where(i == 0, 0, nblk - i)


def _bdiag(lane_masks, x):
    xb = x.astype(BF16)
    per_piece = LANES // RW_HEAD_DIM
    zero = jnp.zeros((x.shape[0], LANES), BF16)
    blocks = []
    for h in range(GROUP_HEADS):
        piece = h // per_piece
        kept = xb[:, piece * LANES:(piece + 1) * LANES] * lane_masks[h % per_piece]
        blocks.append(jnp.concatenate([kept if p == piece else zero for p in range(GROUP_LANES // LANES)], axis=1))
    return jnp.concatenate(blocks, axis=0)


def _stack(top, bottom):
    return jnp.concatenate([top, bottom], axis=0)


def _rwkv_prepare(fwd, pb, nblk, hm_ref, hp_ref, hn_ref, conv_ref, w0, wup, a0, aup, gup_ref, vec_ref,
                  bd_ref, bonus_ref, g_ref, rt_ref, kt_ref, at_ref, bt_ref, v_ref, a2_ref, k2_ref, gc_ref):
    w3 = 3 * RW_WIDTH
    n_chunks = ROW_BLK // CHUNK

    rkv = hm_ref[0, :, 0:w3]
    first = jnp.logical_or(pb == 0, pb == 1)
    last = jnp.logical_or(pb == 0, pb == nblk - 1)
    prev_row = jnp.where(first, 0.0, hp_ref[0, HALO - 1:HALO, :])
    next_row = jnp.where(last, 0.0, hn_ref[0, 0:1, :])
    rows = lax.broadcasted_iota(jnp.int32, (ROW_BLK, 1), 0)
    up = jnp.where(rows == 0, prev_row, pltpu.roll(rkv, 1, 0))
    dn = jnp.where(rows == ROW_BLK - 1, next_row, pltpu.roll(rkv, ROW_BLK - 1, 0))
    rkv = conv_ref[0:1, :] * up + conv_ref[1:2, :] * rkv + conv_ref[2:3, :] * dn
    r = rkv[:, 0:RW_WIDTH]
    k = rkv[:, RW_WIDTH:2 * RW_WIDTH]
    v = rkv[:, 2 * RW_WIDTH:w3]

    dcol = 0 if fwd else 1
    wdn = hm_ref[0, :, w3 + dcol * RW_DECAY_LORA:w3 + (dcol + 1) * RW_DECAY_LORA]
    a_base = w3 + 2 * RW_DECAY_LORA
    adn = hm_ref[0, :, a_base + dcol * RW_AAA_LORA:a_base + (dcol + 1) * RW_AAA_LORA]

    k_k = vec_ref[0:1, :]
    k_a = vec_ref[1:2, :]
    r_k = vec_ref[2:3, :]
    bd = bd_ref[...]

    z = w0 + _mm3(jnp.tanh(wdn), wup)
    ld = (-float(np.exp(-0.5))) * _sigmoid(z)
    a = _sigmoid(a0 + _mm3(adn, aup))
    kd = k * (1.0 + (a - 1.0) * k_a)
    kkr = k * k_k
    kk = kkr * lax.rsqrt(_head_sums(kkr * kkr, bd, 2) + 1e-12)
    bonus_ref[0] = _head_sums(r * kd * r_k, bd, 2) * v
    if g_ref is not None:
        g_ref[0] = _mm3(_sigmoid(hm_ref[0, :, RW_PROJ - RW_GATE_LORA:RW_PROJ]), gup_ref[...])

    ri = lax.broadcasted_iota(jnp.int32, (ROW_BLK, ROW_BLK), 0)
    ci = lax.broadcasted_iota(jnp.int32, (ROW_BLK, ROW_BLK), 1)
    same = (ri & -CHUNK) == (ci & -CHUNK)
    tri = jnp.logical_and(same, (ri >= ci) if fwd else (ri <= ci))
    cl = _mm_sel_lhs3(jnp.where(tri, 1.0, 0.0).astype(BF16), ld)
    end = CHUNK - 1 if fwd else 0
    tot = jnp.concatenate([jnp.broadcast_to(cl[c * CHUNK + end:c * CHUNK + end + 1], (CHUNK, RW_WIDTH))
                           for c in range(n_chunks)], axis=0)
    e_neg = jnp.exp(-cl)
    e_rem = jnp.exp(tot - cl)
    ka = kk * a
    rt_ref[...] = r * jnp.exp(cl)
    kt_ref[...] = kd * e_neg
    at_ref[...] = -ka * e_neg
    bt_ref[...] = kk * jnp.exp(cl - ld)
    v_ref[...] = v
    a2_ref[...] = -ka * e_rem
    k2_ref[...] = kd * e_rem
    gc_ref[...] = jnp.exp(tot)


def _rwkv_chunk_products(lane_masks, directions):
    bdot = functools.partial(_bdot, lane_masks)
    stack = _stack
    ii = lax.broadcasted_iota(jnp.int32, (CHUNK, GROUP_LANES), 0)
    jj = lax.broadcasted_iota(jnp.int32, (CHUNK, GROUP_LANES), 1) & (CHUNK - 1)
    eye = jnp.where(ii == jj, 1.0, 0.0)
    masks = {True: (ii > jj, ii >= jj), False: (ii < jj, ii <= jj)}
    chains = [(refs, (slice(c * CHUNK, (c + 1) * CHUNK), slice(gi * GROUP_LANES, (gi + 1) * GROUP_LANES)))
              for c in range(ROW_BLK // CHUNK) for gi in range(RW_WIDTH // GROUP_LANES) for refs in directions]

    pw, tt, lbkv = {}, {}, {}
    for n, ((fwd, rt_ref, kt_ref, at_ref, bt_ref, v_ref, w_ref, ub_ref, mra_ref, yv_ref), key) in enumerate(chains):
        strict, incl = masks[fwd]
        lhs = stack(bt_ref[key], rt_ref[key])
        ga = bdot(lhs, at_ref[key], trans_b=True)
        gk = bdot(lhs, kt_ref[key], trans_b=True)
        mra_ref[key] = jnp.where(incl, ga[CHUNK:], 0.0)
        pw[n] = jnp.where(strict, ga[:CHUNK], 0.0)
        tt[n] = eye + pw[n]
        lm = stack(jnp.where(strict, gk[:CHUNK], 0.0), jnp.where(incl, gk[CHUNK:], 0.0))
        lv = bdot(lm, v_ref[key])
        lbkv[n] = lv[:CHUNK]
        yv_ref[key] = lv[CHUNK:]
    for n in range(len(chains)):
        pw[n] = bdot(pw[n], pw[n])
    for _ in range(4):
        for n in range(len(chains)):
            st = bdot(stack(pw[n], tt[n]), pw[n])
            pw[n] = st[:CHUNK]
            tt[n] = tt[n] + st[CHUNK:]
    for n, ((fwd, rt_ref, kt_ref, at_ref, bt_ref, v_ref, w_ref, ub_ref, mra_ref, yv_ref), key) in enumerate(chains):
        t_inv = tt[n] + bdot(tt[n], pw[n])
        w_ref[key] = bdot(t_inv, bt_ref[key])
        ub_ref[key] = bdot(t_inv, lbkv[n])


def _rwkv_state_pass(same_head, lane_masks, directions):
    bdot = functools.partial(_bdot, lane_masks)
    n_chunks = ROW_BLK // CHUNK
    for step in range(n_chunks):
        chains = []
        for fwd, y_ref, s_ref, rt_ref, v_ref, a2_ref, k2_ref, gc_ref, w_ref, ub_ref, mra_ref, yv_ref in directions:
            c = step if fwd else n_chunks - 1 - step
            rs = slice(c * CHUNK, (c + 1) * CHUNK)
            for gi in range(RW_WIDTH // GROUP_LANES):
                ls = slice(gi * GROUP_LANES, (gi + 1) * GROUP_LANES)
                chains.append((y_ref, s_ref, rt_ref, v_ref, a2_ref, k2_ref, gc_ref, w_ref, ub_ref, mra_ref, yv_ref,
                               rs, ls, c))
        s0s, wss, us = [], [], []
        for y_ref, s_ref, rt_ref, v_ref, a2_ref, k2_ref, gc_ref, w_ref, ub_ref, mra_ref, yv_ref, rs, ls, c in chains:
            s0 = s_ref[:, ls]
            s0s.append(s0)
            wss.append(bdot(_stack(w_ref[rs, ls], rt_ref[rs, ls]), s0, trans_b=True))
        for n, (y_ref, s_ref, rt_ref, v_ref, a2_ref, k2_ref, gc_ref, w_ref, ub_ref, mra_ref, yv_ref, rs, ls,
                c) in enumerate(chains):
            u = ub_ref[rs, ls] + wss[n][:CHUNK]
            us.append(u)
            uv = jnp.concatenate([u, v_ref[rs, ls]], axis=0)
            ak2 = jnp.concatenate([a2_ref[rs, ls], k2_ref[rs, ls]], axis=0)
            full = jnp.where(same_head, _mm3(uv, ak2, trans_a=True), 0.0)
            inc = full[0:RW_HEAD_DIM]
            for hh in range(1, GROUP_HEADS):
                inc = inc + full[hh * RW_HEAD_DIM:(hh + 1) * RW_HEAD_DIM]
            s_ref[:, ls] = s0s[n] * gc_ref[c * CHUNK:c * CHUNK + 1, ls] + inc
        for n, (y_ref, s_ref, rt_ref, v_ref, a2_ref, k2_ref, gc_ref, w_ref, ub_ref, mra_ref, yv_ref, rs, ls,
                c) in enumerate(chains):
            y_ref[0, rs, ls] = wss[n][CHUNK:] + yv_ref[rs, ls] + bdot(mra_ref[rs, ls], us[n])


RWKV_DIR_SCRATCH = 13


def _rwkv_kernel(hmf_ref, hpf_ref, hnf_ref, hmb_ref, hpb_ref, hnb_ref, conv_ref, w0_ref, wup_ref, a0_ref, aup_ref,
                 gup_ref, vec_ref, bd_ref, yf_ref, yb_ref, bonf_ref, bonb_ref, g_ref, *scratch):
    i = pl.program_id(1)
    nblk = pl.num_programs(1)
    sf, sb = scratch[:RWKV_DIR_SCRATCH], scratch[RWKV_DIR_SCRATCH:]

    @pl.when(i == 0)
    def _():
        sf[0][...] = jnp.zeros_like(sf[0])
        sb[0][...] = jnp.zeros_like(sb[0])

    r4 = lax.broadcasted_iota(jnp.int32, (GROUP_LANES, GROUP_LANES), 0) & -RW_HEAD_DIM
    c4 = lax.broadcasted_iota(jnp.int32, (GROUP_LANES, GROUP_LANES), 1) & -RW_HEAD_DIM
    same_head = r4 == c4
    lane = lax.broadcasted_iota(jnp.int32, (CHUNK, LANES), 1) & -RW_HEAD_DIM
    lane_masks = [jnp.where(lane == j * RW_HEAD_DIM, 1.0, 0.0).astype(BF16) for j in range(LANES // RW_HEAD_DIM)]
    _rwkv_prepare(True, _rwkv_ctx_first(i, nblk, True), nblk, hmf_ref, hpf_ref, hnf_ref, conv_ref,
                  w0_ref[0], wup_ref[0], a0_ref[0], aup_ref[0], gup_ref, vec_ref, bd_ref, bonf_ref, g_ref, *sf[1:9])
    _rwkv_prepare(False, _rwkv_ctx_first(i, nblk, False), nblk, hmb_ref, hpb_ref, hnb_ref, conv_ref,
                  w0_ref[1], wup_ref[1], a0_ref[1], aup_ref[1], gup_ref, vec_ref, bd_ref, bonb_ref, None, *sb[1:9])

    def product_refs(fwd, scr):
        _, rt_ref, kt_ref, at_ref, bt_ref, v_ref, _, _, _, w_ref, ub_ref, mra_ref, yv_ref = scr
        return fwd, rt_ref, kt_ref, at_ref, bt_ref, v_ref, w_ref, ub_ref, mra_ref, yv_ref

    def state_refs(fwd, y_ref, scr):
        s_ref, rt_ref, _, _, _, v_ref, a2_ref, k2_ref, gc_ref, w_ref, ub_ref, mra_ref, yv_ref = scr
        return fwd, y_ref, s_ref, rt_ref, v_ref, a2_ref, k2_ref, gc_ref, w_ref, ub_ref, mra_ref, yv_ref

    _rwkv_chunk_products(lane_masks, [product_refs(True, sf), product_refs(False, sb)])
    _rwkv_state_pass(same_head, lane_masks, [state_refs(True, yf_ref, sf), state_refs(False, yb_ref, sb)])


def _rwkv(hcat, conv, w0, wup, a0, aup, gup, vecs, bd):
    b, tt, _ = hcat.shape
    nblk = tt // ROW_BLK
    t = tt - ROW_BLK
    w3 = 3 * RW_WIDTH
    hb = ROW_BLK // HALO

    def in_specs(fwd):
        pbf = lambda i: _rwkv_ctx_first(i, nblk, fwd)
        return [pl.BlockSpec((1, ROW_BLK, RW_PROJ), lambda bi, i: (bi, pbf(i), 0)),
                pl.BlockSpec((1, HALO, w3), lambda bi, i: (bi, jnp.maximum(pbf(i) * hb - 1, 0), 0)),
                pl.BlockSpec((1, HALO, w3), lambda bi, i: (bi, jnp.minimum((pbf(i) + 1) * hb, nblk * hb - 1), 0))]

    def out_spec(fwd):
        return pl.BlockSpec((1, ROW_BLK, RW_WIDTH),
                            lambda bi, i: (bi, _rwkv_ctx_first(jnp.maximum(i, 1), nblk, fwd) - 1, 0))

    full = lambda a: pl.BlockSpec(a.shape, lambda bi, i: (0,) * a.ndim)
    scr = [pltpu.VMEM((RW_HEAD_DIM, RW_WIDTH), F32)] + [pltpu.VMEM((ROW_BLK, RW_WIDTH), F32)] * (RWKV_DIR_SCRATCH - 1)
    out = jax.ShapeDtypeStruct((b, t, RW_WIDTH), F32)
    return pl.pallas_call(
        _rwkv_kernel,
        out_shape=[out] * 5,
        grid=(b, nblk),
        in_specs=in_specs(True) + in_specs(False) + [full(a) for a in (conv, w0, wup, a0, aup, gup, vecs, bd)],
        out_specs=[out_spec(True), out_spec(False), out_spec(True), out_spec(False), out_spec(True)],
        scratch_shapes=scr + scr,
        compiler_params=_cparams(("arbitrary", "arbitrary")),
        name="rwkv",
    )(hcat, hcat, hcat, hcat, hcat, hcat, conv, w0, wup, a0, aup, gup, vecs, bd)


def _mla_kernel(h_ref, tq_ref, tk_ref, qn_ref, kvn_ref, wq_ref, wkv_ref, q_ref, k_ref, v_ref):
    hq = h_ref[0, :, 0:MLA_Q_LORA]
    hkv = h_ref[0, :, MLA_Q_LORA:MLA_Q_LORA + MLA_KV_LORA]
    kr = h_ref[0, :, MLA_Q_LORA + MLA_KV_LORA:MLA_Q_LORA + MLA_KV_LORA + MLA_ROPE_DIM]
    krs = h_ref[0, :, MLA_Q_LORA + MLA_KV_LORA + MLA_ROPE_DIM:MLA_PROJ_EXT]

    def rms(x, gain):
        return x * lax.rsqrt(jnp.mean(x * x, axis=-1, keepdims=True) + RMS_EPS) * gain

    qall = _mm3(rms(hq, qn_ref[...]), wq_ref[...])
    kvall = _mm3(rms(hkv, kvn_ref[...]), wkv_ref[...])
    nn = MLA_HEADS * MLA_NOPE_DIM
    nr = MLA_HEADS * MLA_ROPE_DIM
    qrot = qall[:, nn:nn + nr] * tq_ref[:, 0:nr] + qall[:, nn + nr:nn + 2 * nr] * tq_ref[:, nr:2 * nr]
    krot = kr * tk_ref[:, 0:MLA_ROPE_DIM] + krs * tk_ref[:, MLA_ROPE_DIM:2 * MLA_ROPE_DIM]
    scale = MLA_QK_DIM ** -0.5
    ones_col = jnp.where(lax.broadcasted_iota(jnp.int32, (ROW_BLK, MLA_V_EXT - MLA_V_DIM), 1) == 0, 1.0, 0.0)
    for h in range(MLA_HEADS):
        qh = jnp.concatenate([qall[:, h * MLA_NOPE_DIM:(h + 1) * MLA_NOPE_DIM],
                              qrot[:, h * MLA_ROPE_DIM:(h + 1) * MLA_ROPE_DIM]], axis=1)
        q_ref[0, h] = qh * scale
        kh = jnp.concatenate([kvall[:, h * MLA_NOPE_DIM:(h + 1) * MLA_NOPE_DIM], krot], axis=1)
        k_ref[0, h] = kh
        vh = kvall[:, nn + h * MLA_V_DIM:nn + (h + 1) * MLA_V_DIM]
        v_ref[0, h] = jnp.concatenate([vh, ones_col], axis=1)


def _mla_qkv(hcat, tabq, tabk, qn, kvn, wq, wkv):
    b, tt, _ = hcat.shape
    t = tt - ROW_BLK
    lat = lambda i: jnp.maximum(i, 1) - 1
    return pl.pallas_call(
        _mla_kernel,
        out_shape=[jax.ShapeDtypeStruct((b, MLA_HEADS, t, MLA_QK_DIM), F32),
                   jax.ShapeDtypeStruct((b, MLA_HEADS, tt, MLA_QK_DIM), F32),
                   jax.ShapeDtypeStruct((b, MLA_HEADS, tt, MLA_V_EXT), F32)],
        grid=(b, tt // ROW_BLK),
        in_specs=[pl.BlockSpec((1, ROW_BLK, MLA_PROJ_EXT), lambda bi, i: (bi, i, RW_PROJ // MLA_PROJ_EXT)),
                  pl.BlockSpec((ROW_BLK, 2 * MLA_HEADS * MLA_ROPE_DIM), lambda bi, i: (lat(i), 0)),
                  pl.BlockSpec((ROW_BLK, 2 * MLA_ROPE_DIM), lambda bi, i: (i, 0)),
                  pl.BlockSpec((1, MLA_Q_LORA), lambda bi, i: (0, 0)),
                  pl.BlockSpec((1, MLA_KV_LORA), lambda bi, i: (0, 0)),
                  pl.BlockSpec(wq.shape, lambda bi, i: (0, 0)),
                  pl.BlockSpec(wkv.shape, lambda bi, i: (0, 0))],
        out_specs=[pl.BlockSpec((1, MLA_HEADS, ROW_BLK, MLA_QK_DIM), lambda bi, i: (bi, 0, lat(i), 0)),
                   pl.BlockSpec((1, MLA_HEADS, ROW_BLK, MLA_QK_DIM), lambda bi, i: (bi, 0, i, 0)),
                   pl.BlockSpec((1, MLA_HEADS, ROW_BLK, MLA_V_EXT), lambda bi, i: (bi, 0, i, 0))],
        compiler_params=_cparams(("arbitrary", "arbitrary")),
        name="mla_qkv",
    )(hcat, tabq, tabk, qn, kvn, wq, wkv)


def _attn_kernel(tk, q_ref, k_ref, v_ref, o_ref):
    q = q_ref[0, 0]
    tq = q.shape[0]
    m = jnp.full((tq, 1), -jnp.inf, F32)
    acc = jnp.zeros((tq, v_ref.shape[3]), F32)
    for j in range(k_ref.shape[2] // tk):
        s = _mm3(q, k_ref[0, 0, j * tk:(j + 1) * tk, :], trans_b=True)
        m_new = jnp.maximum(m, jnp.max(s, axis=-1, keepdims=True))
        p = jnp.exp(s - m_new)
        acc = jnp.exp(m - m_new) * acc + _mm3(p, v_ref[0, 0, j * tk:(j + 1) * tk, :])
        m = m_new
    o_ref[0] = acc[:, :MLA_V_DIM] / acc[:, MLA_V_DIM:MLA_V_DIM + 1]


def _attn_tk(tk_total):
    for cand in (768, 1024, 512, 384, 256):
        if tk_total % cand == 0:
            return cand
    return ROW_BLK


def _attention(q, k, v):
    b, nh, t, dq = q.shape
    tk_total, dv = k.shape[2], v.shape[3]
    tq = min(ATT_TQ, t)
    return pl.pallas_call(
        functools.partial(_attn_kernel, _attn_tk(tk_total)),
        out_shape=jax.ShapeDtypeStruct((b, t, nh * MLA_V_DIM), F32),
        grid=(b, nh, t // tq),
        in_specs=[pl.BlockSpec((1, 1, tq, dq), lambda bi, h, qi: (bi, h, qi, 0)),
                  pl.BlockSpec((1, 1, tk_total, dq), lambda bi, h, qi: (bi, h, 0, 0)),
                  pl.BlockSpec((1, 1, tk_total, dv), lambda bi, h, qi: (bi, h, 0, 0))],
        out_specs=pl.BlockSpec((1, tq, MLA_V_DIM), lambda bi, h, qi: (bi, qi, h)),
        compiler_params=_cparams(("arbitrary", "arbitrary", "arbitrary")),
        name="attn",
    )(q, k, v)


def _mix_kernel(alpha, yf_ref, yb_ref, bonf_ref, bonb_ref, g_ref, o_ref, x_ref, wout_ref, gn_ref, ln_ref, mod_ref,
                rt_ref, bd_ref, x1_ref, u_ref, aff_ref):
    inv_n = 1.0 / RW_HEAD_DIM
    ysum = yf_ref[0] + yb_ref[0]
    stats = _head_sums(jnp.concatenate([ysum, ysum * ysum], axis=0), bd_ref[...], 2) * inv_n
    mean = stats[:ROW_BLK]
    var = stats[ROW_BLK:] - mean * mean
    yn = (ysum - mean) * lax.rsqrt(var + GN_EPS) * gn_ref[0:1, :] + gn_ref[1:2, :]
    rw = (yn + (bonf_ref[0] + bonb_ref[0])) * g_ref[0]
    mix = _mm3(rw, wout_ref[0:RW_WIDTH, :]) + _mm3(o_ref[0], wout_ref[RW_WIDTH:, :])
    g1 = mod_ref[0, 0:1, :]
    sh2 = mod_ref[0, 1:2, :]
    sc2 = mod_ref[0, 2:3, :]
    x1 = _standardize(alpha * x_ref[0] + g1 * mix, LN_EPS) * ln_ref[0:1, :] + ln_ref[1:2, :]
    x1_ref[0] = x1
    u = _standardize(x1, LN_EPS) * (1.0 + sc2) + sh2
    u_ref[0] = u.astype(BF16)
    logits = _mm3(rt_ref[...], u, trans_b=True)
    ex = jnp.exp(logits - jnp.max(logits, axis=0, keepdims=True))
    aff_ref[0] = ex / jnp.sum(ex, axis=0, keepdims=True)


def _mix(alpha, yf, yb, bonf, bonb, g, o_mla, x, wout, gn, ln, modv, router_t, bd):
    b, t, d = x.shape
    ne = router_t.shape[0]
    row = lambda w: pl.BlockSpec((1, ROW_BLK, w), lambda bi, i: (bi, i, 0))
    full = lambda a: pl.BlockSpec(a.shape, lambda bi, i: (0,) * a.ndim)
    return pl.pallas_call(
        functools.partial(_mix_kernel, alpha),
        out_shape=[jax.ShapeDtypeStruct((b, t, d), F32),
                   jax.ShapeDtypeStruct((b, t, d), BF16),
                   jax.ShapeDtypeStruct((b, ne, t), F32)],
        grid=(b, t // ROW_BLK),
        in_specs=[row(RW_WIDTH)] * 5 + [
            row(MLA_HEADS * MLA_V_DIM), row(d), full(wout), full(gn), full(ln),
            pl.BlockSpec((1, 3, d), lambda bi, i: (bi, 0, 0)), full(router_t), full(bd)],
        out_specs=[row(d), row(d), pl.BlockSpec((1, ne, ROW_BLK), lambda bi, i: (bi, 0, i))],
        compiler_params=_cparams(("arbitrary", "arbitrary")),
        name="mix",
    )(yf, yb, bonf, bonb, g, o_mla, x, wout, gn, ln, modv, router_t, bd)


def _topk_kernel(cap, aff_ref, w_ref, pos_ref, cnt_ref):
    aff = aff_ref[0]
    ne, t = aff.shape
    bits = lax.bitcast_convert_type(aff, jnp.int32)

    def count_ge(thr):
        return jnp.sum(jnp.where(bits >= thr, 1.0, 0.0), axis=1, keepdims=True)

    def body(it, thr):
        cand = thr | lax.shift_left(jnp.int32(1), 30 - it)
        return jnp.where(count_ge(cand) >= cap, cand, thr)

    thr = lax.fori_loop(0, 31, body, jnp.zeros((ne, 1), jnp.int32))
    need = cap - jnp.sum(jnp.where(bits > thr, 1.0, 0.0), axis=1, keepdims=True)
    ri = lax.broadcasted_iota(jnp.int32, (LANES, LANES), 0)
    ci = lax.broadcasted_iota(jnp.int32, (LANES, LANES), 1)
    below = jnp.where(ri < ci, 1.0, 0.0).astype(BF16)
    ties = jnp.zeros((ne, 1), F32)
    taken = jnp.zeros((ne, 1), F32)
    for j in range(t // LANES):
        sl = slice(j * LANES, (j + 1) * LANES)
        bj = bits[:, sl]
        eqj = bj == thr
        eqf = jnp.where(eqj, 1.0, 0.0)
        before = pl.dot(eqf.astype(BF16), below) + ties
        take = jnp.logical_or(bj > thr, jnp.logical_and(eqj, before < need))
        takef = jnp.where(take, 1.0, 0.0)
        w_ref[0, :, sl] = jnp.where(take, aff[:, sl], 0.0)
        pos_ref[0, :, sl] = jnp.where(take, pl.dot(takef.astype(BF16), below) + taken, -1.0)
        cnt_ref[0, :, j:j + 1] = taken
        ties = ties + jnp.sum(eqf, axis=1, keepdims=True)
        taken = taken + jnp.sum(takef, axis=1, keepdims=True)


def _topk(aff, cap):
    b, ne, t = aff.shape
    spec = pl.BlockSpec((1, ne, t), lambda bi: (bi, 0, 0))
    return pl.pallas_call(
        functools.partial(_topk_kernel, float(cap)),
        out_shape=[jax.ShapeDtypeStruct((b, ne, t), F32), jax.ShapeDtypeStruct((b, ne, t), F32),
                   jax.ShapeDtypeStruct((b, ne, t // LANES), F32)],
        grid=(b,),
        in_specs=[spec],
        out_specs=[spec, spec, pl.BlockSpec((1, ne, t // LANES), lambda bi: (bi, 0, 0))],
        compiler_params=_cparams(("arbitrary",)),
        name="topk",
    )(aff)


def _moe_ffn_kernel(cap, ws_ref, u_ref, pos_ref, wg_ref, wu_ref, wd_ref, ye_ref, xe_ref, acc_ref):
    bi, e, q = pl.program_id(0), pl.program_id(1), pl.program_id(2)
    ne, nq = pl.num_programs(1), pl.num_programs(2)
    nwin = u_ref.shape[1] // MOE_WIN
    f = wg_ref.shape[2]

    @pl.when(q == 0)
    def _():
        xe_ref[...] = jnp.zeros_like(xe_ref)

    rows = lax.broadcasted_iota(jnp.int32, (MOE_GROWS, 1), 0).astype(F32)
    for wl in range(nwin):
        start = pl.multiple_of(ws_ref[(bi * ne + e) * (nq * nwin) + q * nwin + wl] & -SUBLANES_F32, SUBLANES_F32)
        rank = pos_ref[0, pl.ds(e, 1), wl * MOE_WIN:(wl + 1) * MOE_WIN] - start.astype(F32)
        onehot = jnp.where(rank == rows, 1.0, 0.0).astype(BF16)
        xe_ref[pl.ds(start, MOE_GROWS), :] += pl.dot(onehot, u_ref[0, wl * MOE_WIN:(wl + 1) * MOE_WIN, :])

    @pl.when(q == nq - 1)
    def _():
        xb = xe_ref[0:cap, :].astype(BF16)
        for c in range(f // MOE_FCHUNK):
            cs = slice(c * MOE_FCHUNK, (c + 1) * MOE_FCHUNK)
            hg = pl.dot(xb, wg_ref[0, :, cs])
            hu = pl.dot(xb, wu_ref[0, :, cs])
            part = pl.dot((hg * _sigmoid(hg) * hu).astype(BF16), wd_ref[0, cs, :])
            if c == 0:
                acc_ref[...] = part
            else:
                acc_ref[...] += part
        ye_ref[0, 0, 0:cap, :] = acc_ref[...].astype(BF16)
        ye_ref[0, 0, cap:, :] = jnp.zeros((ye_ref.shape[2] - cap, ye_ref.shape[3]), BF16)


def _moe_ffn(ws, u, pos, wg, wu, wd, cap):
    b, t, d = u.shape
    ne, _, f = wg.shape
    seg = min(MOE_SEG, t)
    cap_pad = cap + MOE_CROWS
    grid_spec = pltpu.PrefetchScalarGridSpec(
        num_scalar_prefetch=1,
        grid=(b, ne, t // seg),
        in_specs=[pl.BlockSpec((1, seg, d), lambda bi, e, q, ws: (bi, q, 0)),
                  pl.BlockSpec((1, ne, seg), lambda bi, e, q, ws: (bi, 0, q)),
                  pl.BlockSpec((1, d, f), lambda bi, e, q, ws: (e, 0, 0)),
                  pl.BlockSpec((1, d, f), lambda bi, e, q, ws: (e, 0, 0)),
                  pl.BlockSpec((1, f, d), lambda bi, e, q, ws: (e, 0, 0))],
        out_specs=pl.BlockSpec((1, 1, cap_pad, d), lambda bi, e, q, ws: (bi, e, 0, 0)),
        scratch_shapes=[pltpu.VMEM((cap + MOE_GROWS, d), F32), pltpu.VMEM((cap, d), F32)])
    return pl.pallas_call(
        functools.partial(_moe_ffn_kernel, cap),
        out_shape=jax.ShapeDtypeStruct((b, ne, cap_pad, d), BF16),
        grid_spec=grid_spec,
        compiler_params=_cparams(("arbitrary", "arbitrary", "arbitrary")),
        name="moe_ffn",
    )(ws, u, pos, wg, wu, wd)


def _moe_combine_kernel(alpha, ne, ws_ref, ye_ref, pos_ref, wt_ref, x1_ref, g2_ref, ln_ref, o_ref, acc_ref):
    bi, qq, s = pl.program_id(0), pl.program_id(1), pl.program_id(2)
    nwin = acc_ref.shape[0] // MOE_CWIN
    total_win = pl.num_programs(1) * nwin

    @pl.when(s == 0)
    def _():
        acc_ref[...] = jnp.zeros_like(acc_ref)

    @pl.when(s < ne)
    def _():
        lane = lax.broadcasted_iota(jnp.int32, (MOE_CWIN, ne), 1)
        cols = lax.broadcasted_iota(jnp.int32, (1, MOE_CROWS), 1).astype(F32)
        for wl in range(nwin):
            ts = slice(wl * MOE_CWIN, (wl + 1) * MOE_CWIN)
            start = pl.multiple_of(ws_ref[(bi * ne + s) * total_win + qq * nwin + wl] & -SUBLANES_BF16, SUBLANES_BF16)
            rank = jnp.sum(jnp.where(lane == s, pos_ref[0, ts, :], 0.0), axis=1, keepdims=True) - start.astype(F32)
            gate = jnp.sum(jnp.where(lane == s, wt_ref[0, ts, :], 0.0), axis=1, keepdims=True)
            onehot = jnp.where(rank == cols, 1.0, 0.0).astype(BF16)
            acc_ref[ts, :] += gate * pl.dot(onehot, ye_ref[0, 0, pl.ds(start, MOE_CROWS), :])

    @pl.when(s >= ne)
    def _():
        r0 = pl.multiple_of((s - ne) * o_ref.shape[1], o_ref.shape[1])
        xr = alpha * x1_ref[0] + g2_ref[0] * acc_ref[pl.ds(r0, o_ref.shape[1]), :]
        o_ref[0] = _standardize(xr, LN_EPS) * ln_ref[0:1, :] + ln_ref[1:2, :]


def _moe_combine(alpha, ws, ye, pos_t, wt_t, x1, g2, ln):
    b, t, d = x1.shape
    ne, cap_pad = ye.shape[1], ye.shape[2]
    seg = min(MOE_SEG, t)
    fin = min(MOE_FIN, seg)
    nfin = seg // fin
    fblk = lambda bi, qq, s, ws: (bi, qq * nfin + jnp.maximum(s - ne, 0), 0)
    grid_spec = pltpu.PrefetchScalarGridSpec(
        num_scalar_prefetch=1,
        grid=(b, t // seg, ne + nfin),
        in_specs=[pl.BlockSpec((1, 1, cap_pad, d), lambda bi, qq, s, ws: (bi, jnp.minimum(s, ne - 1), 0, 0)),
                  pl.BlockSpec((1, seg, ne), lambda bi, qq, s, ws: (bi, qq, 0)),
                  pl.BlockSpec((1, seg, ne), lambda bi, qq, s, ws: (bi, qq, 0)),
                  pl.BlockSpec((1, fin, d), fblk),
                  pl.BlockSpec((1, 1, d), lambda bi, qq, s, ws: (bi, 0, 0)),
                  pl.BlockSpec((2, d), lambda bi, qq, s, ws: (0, 0))],
        out_specs=pl.BlockSpec((1, fin, d), fblk),
        scratch_shapes=[pltpu.VMEM((seg, d), F32)])
    return pl.pallas_call(
        functools.partial(_moe_combine_kernel, alpha, ne),
        out_shape=jax.ShapeDtypeStruct((b, t, d), F32),
        grid_spec=grid_spec,
        compiler_params=_cparams(("arbitrary", "arbitrary", "arbitrary")),
        name="moe_combine",
    )(ws, ye, pos_t, wt_t, x1, g2, ln)


def _rope_tables(t, tc):
    half = MLA_ROPE_DIM // 2
    inv_freq = ROPE_BASE ** (-jnp.arange(0, half, 2, dtype=F32) / half)
    pos = jnp.arange(t)
    ang_r = (pos // GRID_W).reshape(-1, 1).astype(F32) * inv_freq
    ang_c = (pos % GRID_W).reshape(-1, 1).astype(F32) * inv_freq
    cos = jnp.concatenate([jnp.cos(ang_r)] * 2 + [jnp.cos(ang_c)] * 2, axis=-1)
    sin = jnp.concatenate([-jnp.sin(ang_r), jnp.sin(ang_r), -jnp.sin(ang_c), jnp.sin(ang_c)], axis=-1)
    tabq = jnp.concatenate([jnp.tile(cos, (1, MLA_HEADS)), jnp.tile(sin, (1, MLA_HEADS))], axis=-1)
    cos_k = jnp.concatenate([jnp.ones((tc, MLA_ROPE_DIM), F32), cos], axis=0)
    sin_k = jnp.concatenate([jnp.zeros((tc, MLA_ROPE_DIM), F32), sin], axis=0)
    return tabq, jnp.concatenate([cos_k, sin_k], axis=-1)


def _pair_swap():
    q = MLA_ROPE_DIM // 4
    return np.concatenate([np.arange(q, 2 * q), np.arange(0, q), np.arange(3 * q, 4 * q), np.arange(2 * q, 3 * q)])


def kernel(x, c, ctx, c_ctx, w_ada, b_ada, w_in, rwkv_conv, rwkv_w0, rwkv_w_up, rwkv_a0, rwkv_a_up, rwkv_g_up, rwkv_k_k, rwkv_k_a, rwkv_r_k, rwkv_gn_g, rwkv_gn_b, mla_q_norm, mla_w_uq, mla_kv_norm, mla_w_uk, mla_w_uv, w_out, ln1_g, ln1_b, router, exp_w_gate, exp_w_up, exp_w_down, ln2_g, ln2_b):
    b, t, d = x.shape
    tc = ctx.shape[1]
    depth = w_ada.shape[0]
    assert depth == 1 and tc == ROW_BLK and t % ROW_BLK == 0 and t % LANES == 0
    alpha = (2.0 * depth) ** 0.25
    cap = CAPACITY_FACTOR * t // N_EXPERTS

    pad = (-(b + 1)) % 8
    cc = jnp.concatenate([c, c_ctx[None, :], jnp.zeros((pad, d), F32)], axis=0)
    mod = _ada(cc, w_ada[0], b_ada)
    sh1, sc1, g1, sh2, sc2, g2 = jnp.split(mod, 6, axis=-1)
    ss_lat = jnp.stack([sh1[:b], sc1[:b]], axis=1)
    ss_ctx = jnp.broadcast_to(jnp.stack([sh1[b], sc1[b]], axis=0)[None], (b, 2, d))
    ss = jnp.stack([ss_ctx, ss_lat], axis=1)

    swap = _pair_swap()
    kr0 = RW_PROJ + MLA_Q_LORA + MLA_KV_LORA
    w_ext = jnp.concatenate([w_in[0], w_in[0][:, kr0 + swap]], axis=1)
    hcat = _inproj(ctx, x, ss, w_ext)

    head_ones = np.kron(np.eye(RW_HEADS, dtype=np.float32), np.ones((RW_HEAD_DIM, RW_HEAD_DIM), np.float32))
    bd = jnp.asarray(head_ones, BF16)
    vecs = jnp.stack([rwkv_k_k[0], rwkv_k_a[0], rwkv_r_k[0].reshape(-1)], axis=0)
    yf, yb, bonf, bonb, gate = _rwkv(hcat, rwkv_conv[0], rwkv_w0[0][:, None, :], rwkv_w_up[0],
                                     rwkv_a0[0][:, None, :], rwkv_a_up[0], rwkv_g_up[0], vecs,
                                     bd[:GROUP_LANES, :GROUP_LANES])

    wq = mla_w_uq[0].reshape(MLA_Q_LORA, MLA_HEADS, MLA_QK_DIM)
    wq_nope = wq[:, :, :MLA_NOPE_DIM].reshape(MLA_Q_LORA, -1)
    wq_rope = wq[:, :, MLA_NOPE_DIM:]
    wq_ext = jnp.concatenate([wq_nope, wq_rope.reshape(MLA_Q_LORA, -1),
                              wq_rope[:, :, swap].reshape(MLA_Q_LORA, -1)], axis=1)
    wkv = jnp.concatenate([mla_w_uk[0], mla_w_uv[0]], axis=1)
    tabq, tabk = _rope_tables(t, tc)
    q, k, v = _mla_qkv(hcat, tabq, tabk, mla_q_norm, mla_kv_norm, wq_ext, wkv)
    o_mla = _attention(q, k, v)

    gn = jnp.stack([rwkv_gn_g[0], rwkv_gn_b[0]], axis=0)
    ln1 = jnp.stack([ln1_g[0], ln1_b[0]], axis=0)
    modv = jnp.stack([g1[:b], sh2[:b], sc2[:b]], axis=1)
    x1, u, aff = _mix(alpha, yf, yb, bonf, bonb, gate, o_mla, x, w_out[0], gn, ln1, modv, router[0].T, bd[:GROUP_LANES, :GROUP_LANES])

    wt, pos, cnt = _topk(aff, cap)
    cnt = cnt.astype(jnp.int32)
    ws = cnt[:, :, ::MOE_WIN // LANES].reshape(-1)
    ws_c = cnt[:, :, ::MOE_CWIN // LANES].reshape(-1)
    ye = _moe_ffn(ws, u, pos, exp_w_gate[0].astype(BF16), exp_w_up[0].astype(BF16), exp_w_down[0].astype(BF16), cap)
    ln2 = jnp.stack([ln2_g[0], ln2_b[0]], axis=0)
    return _moe_combine(alpha, ws_c, ye, jnp.swapaxes(pos, 1, 2), jnp.swapaxes(wt, 1, 2), x1, g2[:b, None, :], ln2)
```

```python
import functools

import jax
import jax.numpy as jnp
import numpy as np
from jax import lax
from jax.experimental import pallas as pl
from jax.experimental.pallas import tpu as pltpu

F32 = jnp.float32
BF16 = jnp.bfloat16

GRID_W = 64
RW_HEADS = 8
RW_HEAD_DIM = 64
RW_WIDTH = RW_HEADS * RW_HEAD_DIM
RW_DECAY_LORA = 64
RW_AAA_LORA = 64
RW_GATE_LORA = 128
MLA_HEADS = 4
MLA_Q_LORA = 256
MLA_KV_LORA = 256
MLA_NOPE_DIM = 128
MLA_ROPE_DIM = 64
MLA_V_DIM = 128
MLA_QK_DIM = MLA_NOPE_DIM + MLA_ROPE_DIM
MLA_V_EXT = 256
ROPE_BASE = 10000.0
N_EXPERTS = 16
CAPACITY_FACTOR = 2
LN_EPS = 1e-5
RMS_EPS = 1e-6
GN_EPS = 64e-5
RW_PROJ = 3 * RW_WIDTH + 2 * RW_DECAY_LORA + 2 * RW_AAA_LORA + RW_GATE_LORA
MLA_PROJ_EXT = MLA_Q_LORA + MLA_KV_LORA + 2 * MLA_ROPE_DIM
IN_PROJ_EXT = RW_PROJ + MLA_PROJ_EXT

ROW_BLK = 256
CHUNK = 64
LANES = 128
HALO = 8
GROUP_LANES = 256
GROUP_HEADS = GROUP_LANES // RW_HEAD_DIM
ATT_TQ = 1024
MOE_WIN = 256
SUBLANES_F32 = 8
SUBLANES_BF16 = 16
MOE_GROWS = MOE_WIN + SUBLANES_F32
MOE_CWIN = 128
MOE_CROWS = MOE_CWIN + SUBLANES_BF16
MOE_SEG = 2048
MOE_FIN = 512
MOE_FCHUNK = 256
VMEM_LIMIT = 48 * 1024 * 1024


def _cparams(sem):
    return pltpu.CompilerParams(dimension_semantics=sem, vmem_limit_bytes=VMEM_LIMIT)


def _split2(x):
    hi = x.astype(BF16)
    lo = (x - hi.astype(F32)).astype(BF16)
    return hi, lo


def _split3(x):
    hi = x.astype(BF16)
    r = x - hi.astype(F32)
    mid = r.astype(BF16)
    lo = (r - mid.astype(F32)).astype(BF16)
    return hi, mid, lo


def _mm3(a, b, trans_a=False, trans_b=False):
    ah, al = _split2(a)
    bh, bl = _split2(b)
    d = functools.partial(pl.dot, trans_a=trans_a, trans_b=trans_b)
    return d(ah, bh) + (d(ah, bl) + d(al, bh))


def _bdot(lane_masks, a, x, trans_b=False):
    ah, al = _split2(a)
    xh, xl = _split2(x)
    bh = _bdiag(lane_masks, xh)
    d = functools.partial(pl.dot, trans_b=trans_b)
    return d(ah, bh) + (d(ah, _bdiag(lane_masks, xl)) + d(al, bh))


def _head_sums(x, group_ones, passes):
    parts = _split2(x)
    out = []
    for gi in range(x.shape[1] // GROUP_LANES):
        ls = slice(gi * GROUP_LANES, (gi + 1) * GROUP_LANES)
        acc = pl.dot(parts[0][:, ls], group_ones)
        if passes == 2:
            acc = acc + pl.dot(parts[1][:, ls], group_ones)
        out.append(acc)
    return jnp.concatenate(out, axis=1)


def _mm_sel_lhs3(sel, b):
    bh, bm, bl = _split3(b)
    return pl.dot(sel, bh) + (pl.dot(sel, bm) + pl.dot(sel, bl))


def _sigmoid(x):
    return 1.0 / (1.0 + jnp.exp(-x))


def _standardize(x, eps):
    mu = jnp.mean(x, axis=-1, keepdims=True)
    xc = x - mu
    var = jnp.mean(xc * xc, axis=-1, keepdims=True)
    return xc * lax.rsqrt(var + eps)


def _ada_kernel(c_ref, w_ref, b_ref, o_ref):
    c = c_ref[...]
    s = c * _sigmoid(c)
    o_ref[...] = _mm3(s, w_ref[...]) + b_ref[...]


def _ada(cc, w, b):
    rows, d = cc.shape
    n = w.shape[1]
    return pl.pallas_call(
        _ada_kernel,
        out_shape=jax.ShapeDtypeStruct((rows, n), F32),
        grid=(n // d,),
        in_specs=[pl.BlockSpec((rows, d), lambda j: (0, 0)),
                  pl.BlockSpec((d, d), lambda j: (0, j)),
                  pl.BlockSpec((1, d), lambda j: (0, j))],
        out_specs=pl.BlockSpec((rows, d), lambda j: (0, j)),
        compiler_params=_cparams(("arbitrary",)),
        name="ada",
    )(cc, w, b)


def _inproj_kernel(ctx_ref, x_ref, ss_ref, w_ref, o_ref):
    xn = _standardize(jnp.where(pl.program_id(1) == 0, ctx_ref[0], x_ref[0]), LN_EPS)
    shift = ss_ref[0, 0, 0:1, :]
    scale = ss_ref[0, 0, 1:2, :]
    o_ref[0] = _mm3(xn * (1.0 + scale) + shift, w_ref[...])


def _inproj(ctx, x, ss, w_ext):
    b, t, d = x.shape
    tt = t + ctx.shape[1]
    n = w_ext.shape[1]
    return pl.pallas_call(
        _inproj_kernel,
        out_shape=jax.ShapeDtypeStruct((b, tt, n), F32),
        grid=(b, tt // ROW_BLK),
        in_specs=[pl.BlockSpec((1, ROW_BLK, d), lambda bi, i: (bi, 0, 0)),
                  pl.BlockSpec((1, ROW_BLK, d), lambda bi, i: (bi, jnp.maximum(i, 1) - 1, 0)),
                  pl.BlockSpec((1, 1, 2, d), lambda bi, i: (bi, jnp.minimum(i, 1), 0, 0)),
                  pl.BlockSpec((d, n), lambda bi, i: (0, 0))],
        out_specs=pl.BlockSpec((1, ROW_BLK, n), lambda bi, i: (bi, i, 0)),
        compiler_params=_cparams(("arbitrary", "arbitrary")),
        name="inproj",
    )(ctx, x, ss, w_ext)


def _rwkv_ctx_first(i, nblk, fwd):
    return i if fwd else jnp.where(i == 0, 0, nblk - i)


def _bdiag(lane_masks, x):
    xb = x.astype(BF16)
    per_piece = LANES // RW_HEAD_DIM
    zero = jnp.zeros((x.shape[0], LANES), BF16)
    blocks = []
    for h in range(GROUP_HEADS):
        piece = h // per_piece
        kept = xb[:, piece * LANES:(piece + 1) * LANES] * lane_masks[h % per_piece]
        blocks.append(jnp.concatenate([kept if p == piece else zero for p in range(GROUP_LANES // LANES)], axis=1))
    return jnp.concatenate(blocks, axis=0)


def _stack(top, bottom):
    return jnp.concatenate([top, bottom], axis=0)


def _rwkv_prepare(fwd, pb, nblk, hm_ref, hp_ref, hn_ref, conv_ref, w0, wup, a0, aup, gup_ref, vec_ref,
                  bd_ref, bonus_ref, g_ref, rt_ref, kt_ref, at_ref, bt_ref, v_ref, a2_ref, k2_ref, gc_ref):
    w3 = 3 * RW_WIDTH
    n_chunks = ROW_BLK // CHUNK

    rkv = hm_ref[0, :, 0:w3]
    first = jnp.logical_or(pb == 0, pb == 1)
    last = jnp.logical_or(pb == 0, pb == nblk - 1)
    prev_row = jnp.where(first, 0.0, hp_ref[0, HALO - 1:HALO, :])
    next_row = jnp.where(last, 0.0, hn_ref[0, 0:1, :])
    rows = lax.broadcasted_iota(jnp.int32, (ROW_BLK, 1), 0)
    up = jnp.where(rows == 0, prev_row, pltpu.roll(rkv, 1, 0))
    dn = jnp.where(rows == ROW_BLK - 1, next_row, pltpu.roll(rkv, ROW_BLK - 1, 0))
    rkv = conv_ref[0:1, :] * up + conv_ref[1:2, :] * rkv + conv_ref[2:3, :] * dn
    r = rkv[:, 0:RW_WIDTH]
    k = rkv[:, RW_WIDTH:2 * RW_WIDTH]
    v = rkv[:, 2 * RW_WIDTH:w3]

    dcol = 0 if fwd else 1
    wdn = hm_ref[0, :, w3 + dcol * RW_DECAY_LORA:w3 + (dcol + 1) * RW_DECAY_LORA]
    a_base = w3 + 2 * RW_DECAY_LORA
    adn = hm_ref[0, :, a_base + dcol * RW_AAA_LORA:a_base + (dcol + 1) * RW_AAA_LORA]

    k_k = vec_ref[0:1, :]
    k_a = vec_ref[1:2, :]
    r_k = vec_ref[2:3, :]
    bd = bd_ref[...]

    z = w0 + _mm3(jnp.tanh(wdn), wup)
    ld = (-float(np.exp(-0.5))) * _sigmoid(z)
    a = _sigmoid(a0 + _mm3(adn, aup))
    kd = k * (1.0 + (a - 1.0) * k_a)
    kkr = k * k_k
    kk = kkr * lax.rsqrt(_head_sums(kkr * kkr, bd, 2) + 1e-12)
    bonus_ref[0] = _head_sums(r * kd * r_k, bd, 2) * v
    if g_ref is not None:
        g_ref[0] = _mm3(_sigmoid(hm_ref[0, :, RW_PROJ - RW_GATE_LORA:RW_PROJ]), gup_ref[...])

    ri = lax.broadcasted_iota(jnp.int32, (ROW_BLK, ROW_BLK), 0)
    ci = lax.broadcasted_iota(jnp.int32, (ROW_BLK, ROW_BLK), 1)
    same = (ri & -CHUNK) == (ci & -CHUNK)
    tri = jnp.logical_and(same, (ri >= ci) if fwd else (ri <= ci))
    cl = _mm_sel_lhs3(jnp.where(tri, 1.0, 0.0).astype(BF16), ld)
    end = CHUNK - 1 if fwd else 0
    tot = jnp.concatenate([jnp.broadcast_to(cl[c * CHUNK + end:c * CHUNK + end + 1], (CHUNK, RW_WIDTH))
                           for c in range(n_chunks)], axis=0)
    e_neg = jnp.exp(-cl)
    e_rem = jnp.exp(tot - cl)
    ka = kk * a
    rt_ref[...] = r * jnp.exp(cl)
    kt_ref[...] = kd * e_neg
    at_ref[...] = -ka * e_neg
    bt_ref[...] = kk * jnp.exp(cl - ld)
    v_ref[...] = v
    a2_ref[...] = -ka * e_rem
    k2_ref[...] = kd * e_rem
    gc_ref[...] = jnp.exp(tot)


def _rwkv_chunk_products(lane_masks, directions):
    bdot = functools.partial(_bdot, lane_masks)
    stack = _stack
    ii = lax.broadcasted_iota(jnp.int32, (CHUNK, GROUP_LANES), 0)
    jj = lax.broadcasted_iota(jnp.int32, (CHUNK, GROUP_LANES), 1) & (CHUNK - 1)
    eye = jnp.where(ii == jj, 1.0, 0.0)
    masks = {True: (ii > jj, ii >= jj), False: (ii < jj, ii <= jj)}
    chains = [(refs, (slice(c * CHUNK, (c + 1) * CHUNK), slice(gi * GROUP_LANES, (gi + 1) * GROUP_LANES)))
              for c in range(ROW_BLK // CHUNK) for gi in range(RW_WIDTH // GROUP_LANES) for refs in directions]

    pw, tt, lbkv = {}, {}, {}
    for n, ((fwd, rt_ref, kt_ref, at_ref, bt_ref, v_ref, w_ref, ub_ref, mra_ref, yv_ref), key) in enumerate(chains):
        strict, incl = masks[fwd]
        lhs = stack(bt_ref[key], rt_ref[key])
        ga = bdot(lhs, at_ref[key], trans_b=True)
        gk = bdot(lhs, kt_ref[key], trans_b=True)
        mra_ref[key] = jnp.where(incl, ga[CHUNK:], 0.0)
        pw[n] = jnp.where(strict, ga[:CHUNK], 0.0)
        tt[n] = eye + pw[n]
        lm = stack(jnp.where(strict, gk[:CHUNK], 0.0), jnp.where(incl, gk[CHUNK:], 0.0))
        lv = bdot(lm, v_ref[key])
        lbkv[n] = lv[:CHUNK]
        yv_ref[key] = lv[CHUNK:]
    for n in range(len(chains)):
        pw[n] = bdot(pw[n], pw[n])
    for _ in range(4):
        for n in range(len(chains)):
            st = bdot(stack(pw[n], tt[n]), pw[n])
            pw[n] = st[:CHUNK]
            tt[n] = tt[n] + st[CHUNK:]
    for n, ((fwd, rt_ref, kt_ref, at_ref, bt_ref, v_ref, w_ref, ub_ref, mra_ref, yv_ref), key) in enumerate(chains):
        t_inv = tt[n] + bdot(tt[n], pw[n])
        w_ref[key] = bdot(t_inv, bt_ref[key])
        ub_ref[key] = bdot(t_inv, lbkv[n])


def _rwkv_state_pass(same_head, lane_masks, directions):
    bdot = functools.partial(_bdot, lane_masks)
    n_chunks = ROW_BLK // CHUNK
    for step in range(n_chunks):
        chains = []
        for fwd, y_ref, s_ref, rt_ref, v_ref, a2_ref, k2_ref, gc_ref, w_ref, ub_ref, mra_ref, yv_ref in directions:
            c = step if fwd else n_chunks - 1 - step
            rs = slice(c * CHUNK, (c + 1) * CHUNK)
            for gi in range(RW_WIDTH // GROUP_LANES):
                ls = slice(gi * GROUP_LANES, (gi + 1) * GROUP_LANES)
                chains.append((y_ref, s_ref, rt_ref, v_ref, a2_ref, k2_ref, gc_ref, w_ref, ub_ref, mra_ref, yv_ref,
                               rs, ls, c))
        s0s, wss, us = [], [], []
        for y_ref, s_ref, rt_ref, v_ref, a2_ref, k2_ref, gc_ref, w_ref, ub_ref, mra_ref, yv_ref, rs, ls, c in chains:
            s0 = s_ref[:, ls]
            s0s.append(s0)
            wss.append(bdot(_stack(w_ref[rs, ls], rt_ref[rs, ls]), s0, trans_b=True))
        for n, (y_ref, s_ref, rt_ref, v_ref, a2_ref, k2_ref, gc_ref, w_ref, ub_ref, mra_ref, yv_ref, rs, ls,
                c) in enumerate(chains):
            u = ub_ref[rs, ls] + wss[n][:CHUNK]
            us.append(u)
            uv = jnp.concatenate([u, v_ref[rs, ls]], axis=0)
            ak2 = jnp.concatenate([a2_ref[rs, ls], k2_ref[rs, ls]], axis=0)
            full = jnp.where(same_head, _mm3(uv, ak2, trans_a=True), 0.0)
            inc = full[0:RW_HEAD_DIM]
            for hh in range(1, GROUP_HEADS):
                inc = inc + full[hh * RW_HEAD_DIM:(hh + 1) * RW_HEAD_DIM]
            s_ref[:, ls] = s0s[n] * gc_ref[c * CHUNK:c * CHUNK + 1, ls] + inc
        for n, (y_ref, s_ref, rt_ref, v_ref, a2_ref, k2_ref, gc_ref, w_ref, ub_ref, mra_ref, yv_ref, rs, ls,
                c) in enumerate(chains):
            y_ref[0, rs, ls] = wss[n][CHUNK:] + yv_ref[rs, ls] + bdot(mra_ref[rs, ls], us[n])


RWKV_DIR_SCRATCH = 13


def _rwkv_kernel(hmf_ref, hpf_ref, hnf_ref, hmb_ref, hpb_ref, hnb_ref, conv_ref, w0_ref, wup_ref, a0_ref, aup_ref,
                 gup_ref, vec_ref, bd_ref, yf_ref, yb_ref, bonf_ref, bonb_ref, g_ref, *scratch):
    i = pl.program_id(1)
    nblk = pl.num_programs(1)
    sf, sb = scratch[:RWKV_DIR_SCRATCH], scratch[RWKV_DIR_SCRATCH:]

    @pl.when(i == 0)
    def _():
        sf[0][...] = jnp.zeros_like(sf[0])
        sb[0][...] = jnp.zeros_like(sb[0])

    r4 = lax.broadcasted_iota(jnp.int32, (GROUP_LANES, GROUP_LANES), 0) & -RW_HEAD_DIM
    c4 = lax.broadcasted_iota(jnp.int32, (GROUP_LANES, GROUP_LANES), 1) & -RW_HEAD_DIM
    same_head = r4 == c4
    lane = lax.broadcasted_iota(jnp.int32, (CHUNK, LANES), 1) & -RW_HEAD_DIM
    lane_masks = [jnp.where(lane == j * RW_HEAD_DIM, 1.0, 0.0).astype(BF16) for j in range(LANES // RW_HEAD_DIM)]
    _rwkv_prepare(True, _rwkv_ctx_first(i, nblk, True), nblk, hmf_ref, hpf_ref, hnf_ref, conv_ref,
                  w0_ref[0], wup_ref[0], a0_ref[0], aup_ref[0], gup_ref, vec_ref, bd_ref, bonf_ref, g_ref, *sf[1:9])
    _rwkv_prepare(False, _rwkv_ctx_first(i, nblk, False), nblk, hmb_ref, hpb_ref, hnb_ref, conv_ref,
                  w0_ref[1], wup_ref[1], a0_ref[1], aup_ref[1], gup_ref, vec_ref, bd_ref, bonb_ref, None, *sb[1:9])

    def product_refs(fwd, scr):
        _, rt_ref, kt_ref, at_ref, bt_ref, v_ref, _, _, _, w_ref, ub_ref, mra_ref, yv_ref = scr
        return fwd, rt_ref, kt_ref, at_ref, bt_ref, v_ref, w_ref, ub_ref, mra_ref, yv_ref

    def state_refs(fwd, y_ref, scr):
        s_ref, rt_ref, _, _, _, v_ref, a2_ref, k2_ref, gc_ref, w_ref, ub_ref, mra_ref, yv_ref = scr
        return fwd, y_ref, s_ref, rt_ref, v_ref, a2_ref, k2_ref, gc_ref, w_ref, ub_ref, mra_ref, yv_ref

    _rwkv_chunk_products(lane_masks, [product_refs(True, sf), product_refs(False, sb)])
    _rwkv_state_pass(same_head, lane_masks, [state_refs(True, yf_ref, sf), state_refs(False, yb_ref, sb)])


def _rwkv(hcat, conv, w0, wup, a0, aup, gup, vecs, bd):
    b, tt, _ = hcat.shape
    nblk = tt // ROW_BLK
    t = tt - ROW_BLK
    w3 = 3 * RW_WIDTH
    hb = ROW_BLK // HALO

    def in_specs(fwd):
        pbf = lambda i: _rwkv_ctx_first(i, nblk, fwd)
        return [pl.BlockSpec((1, ROW_BLK, RW_PROJ), lambda bi, i: (bi, pbf(i), 0)),
                pl.BlockSpec((1, HALO, w3), lambda bi, i: (bi, jnp.maximum(pbf(i) * hb - 1, 0), 0)),
                pl.BlockSpec((1, HALO, w3), lambda bi, i: (bi, jnp.minimum((pbf(i) + 1) * hb, nblk * hb - 1), 0))]

    def out_spec(fwd):
        return pl.BlockSpec((1, ROW_BLK, RW_WIDTH),
                            lambda bi, i: (bi, _rwkv_ctx_first(jnp.maximum(i, 1), nblk, fwd) - 1, 0))

    full = lambda a: pl.BlockSpec(a.shape, lambda bi, i: (0,) * a.ndim)
    scr = [pltpu.VMEM((RW_HEAD_DIM, RW_WIDTH), F32)] + [pltpu.VMEM((ROW_BLK, RW_WIDTH), F32)] * (RWKV_DIR_SCRATCH - 1)
    out = jax.ShapeDtypeStruct((b, t, RW_WIDTH), F32)
    return pl.pallas_call(
        _rwkv_kernel,
        out_shape=[out] * 5,
        grid=(b, nblk),
        in_specs=in_specs(True) + in_specs(False) + [full(a) for a in (conv, w0, wup, a0, aup, gup, vecs, bd)],
        out_specs=[out_spec(True), out_spec(False), out_spec(True), out_spec(False), out_spec(True)],
        scratch_shapes=scr + scr,
        compiler_params=_cparams(("arbitrary", "arbitrary")),
        name="rwkv",
    )(hcat, hcat, hcat, hcat, hcat, hcat, conv, w0, wup, a0, aup, gup, vecs, bd)


def _mla_kernel(h_ref, tq_ref, tk_ref, qn_ref, kvn_ref, wq_ref, wkv_ref, q_ref, k_ref, v_ref):
    hq = h_ref[0, :, 0:MLA_Q_LORA]
    hkv = h_ref[0, :, MLA_Q_LORA:MLA_Q_LORA + MLA_KV_LORA]
    kr = h_ref[0, :, MLA_Q_LORA + MLA_KV_LORA:MLA_Q_LORA + MLA_KV_LORA + MLA_ROPE_DIM]
    krs = h_ref[0, :, MLA_Q_LORA + MLA_KV_LORA + MLA_ROPE_DIM:MLA_PROJ_EXT]

    def rms(x, gain):
        return x * lax.rsqrt(jnp.mean(x * x, axis=-1, keepdims=True) + RMS_EPS) * gain

    qall = _mm3(rms(hq, qn_ref[...]), wq_ref[...])
    kvall = _mm3(rms(hkv, kvn_ref[...]), wkv_ref[...])
    nn = MLA_HEADS * MLA_NOPE_DIM
    nr = MLA_HEADS * MLA_ROPE_DIM
    qrot = qall[:, nn:nn + nr] * tq_ref[:, 0:nr] + qall[:, nn + nr:nn + 2 * nr] * tq_ref[:, nr:2 * nr]
    krot = kr * tk_ref[:, 0:MLA_ROPE_DIM] + krs * tk_ref[:, MLA_ROPE_DIM:2 * MLA_ROPE_DIM]
    scale = MLA_QK_DIM ** -0.5
    ones_col = jnp.where(lax.broadcasted_iota(jnp.int32, (ROW_BLK, MLA_V_EXT - MLA_V_DIM), 1) == 0, 1.0, 0.0)
    for h in range(MLA_HEADS):
        qh = jnp.concatenate([qall[:, h * MLA_NOPE_DIM:(h + 1) * MLA_NOPE_DIM],
                              qrot[:, h * MLA_ROPE_DIM:(h + 1) * MLA_ROPE_DIM]], axis=1)
        q_ref[0, h] = qh * scale
        kh = jnp.concatenate([kvall[:, h * MLA_NOPE_DIM:(h + 1) * MLA_NOPE_DIM], krot], axis=1)
        k_ref[0, h] = kh.astype(BF16)
        vh = kvall[:, nn + h * MLA_V_DIM:nn + (h + 1) * MLA_V_DIM]
        v_ref[0, h] = jnp.concatenate([vh, ones_col], axis=1).astype(BF16)


def _mla_qkv(hcat, tabq, tabk, qn, kvn, wq, wkv):
    b, tt, _ = hcat.shape
    t = tt - ROW_BLK
    lat = lambda i: jnp.maximum(i, 1) - 1
    return pl.pallas_call(
        _mla_kernel,
        out_shape=[jax.ShapeDtypeStruct((b, MLA_HEADS, t, MLA_QK_DIM), F32),
                   jax.ShapeDtypeStruct((b, MLA_HEADS, tt, MLA_QK_DIM), BF16),
                   jax.ShapeDtypeStruct((b, MLA_HEADS, tt, MLA_V_EXT), BF16)],
        grid=(b, tt // ROW_BLK),
        in_specs=[pl.BlockSpec((1, ROW_BLK, MLA_PROJ_EXT), lambda bi, i: (bi, i, RW_PROJ // MLA_PROJ_EXT)),
                  pl.BlockSpec((ROW_BLK, 2 * MLA_HEADS * MLA_ROPE_DIM), lambda bi, i: (lat(i), 0)),
                  pl.BlockSpec((ROW_BLK, 2 * MLA_ROPE_DIM), lambda bi, i: (i, 0)),
                  pl.BlockSpec((1, MLA_Q_LORA), lambda bi, i: (0, 0)),
                  pl.BlockSpec((1, MLA_KV_LORA), lambda bi, i: (0, 0)),
                  pl.BlockSpec(wq.shape, lambda bi, i: (0, 0)),
                  pl.BlockSpec(wkv.shape, lambda bi, i: (0, 0))],
        out_specs=[pl.BlockSpec((1, MLA_HEADS, ROW_BLK, MLA_QK_DIM), lambda bi, i: (bi, 0, lat(i), 0)),
                   pl.BlockSpec((1, MLA_HEADS, ROW_BLK, MLA_QK_DIM), lambda bi, i: (bi, 0, i, 0)),
                   pl.BlockSpec((1, MLA_HEADS, ROW_BLK, MLA_V_EXT), lambda bi, i: (bi, 0, i, 0))],
        compiler_params=_cparams(("arbitrary", "arbitrary")),
        name="mla_qkv",
    )(hcat, tabq, tabk, qn, kvn, wq, wkv)


def _attn_kernel(tk, q_ref, k_ref, v_ref, o_ref):
    q_hi, q_lo = _split2(q_ref[0, 0])
    tq = q_hi.shape[0]
    m = jnp.full((tq, 1), -jnp.inf, F32)
    acc = jnp.zeros((tq, v_ref.shape[3]), F32)
    for j in range(k_ref.shape[2] // tk):
        kc = k_ref[0, 0, j * tk:(j + 1) * tk, :]
        s = pl.dot(q_hi, kc, trans_b=True) + pl.dot(q_lo, kc, trans_b=True)
        m_new = jnp.maximum(m, jnp.max(s, axis=-1, keepdims=True))
        p = jnp.exp(s - m_new).astype(BF16)
        acc = jnp.exp(m - m_new) * acc + pl.dot(p, v_ref[0, 0, j * tk:(j + 1) * tk, :])
        m = m_new
    o_ref[0] = acc[:, :MLA_V_DIM] / acc[:, MLA_V_DIM:MLA_V_DIM + 1]


def _attn_tk(tk_total):
    for cand in (768, 1024, 512, 384, 256):
        if tk_total % cand == 0:
            return cand
    return ROW_BLK


def _attention(q, k, v):
    b, nh, t, dq = q.shape
    tk_total, dv = k.shape[2], v.shape[3]
    tq = min(ATT_TQ, t)
    return pl.pallas_call(
        functools.partial(_attn_kernel, _attn_tk(tk_total)),
        out_shape=jax.ShapeDtypeStruct((b, t, nh * MLA_V_DIM), F32),
        grid=(b, nh, t // tq),
        in_specs=[pl.BlockSpec((1, 1, tq, dq), lambda bi, h, qi: (bi, h, qi, 0)),
                  pl.BlockSpec((1, 1, tk_total, dq), lambda bi, h, qi: (bi, h, 0, 0)),
                  pl.BlockSpec((1, 1, tk_total, dv), lambda bi, h, qi: (bi, h, 0, 0))],
        out_specs=pl.BlockSpec((1, tq, MLA_V_DIM), lambda bi, h, qi: (bi, qi, h)),
        compiler_params=_cparams(("arbitrary", "arbitrary", "arbitrary")),
        name="attn",
    )(q, k, v)


def _mix_kernel(alpha, yf_ref, yb_ref, bonf_ref, bonb_ref, g_ref, o_ref, x_ref, wout_ref, gn_ref, ln_ref, mod_ref,
                rt_ref, bd_ref, x1_ref, u_ref, aff_ref):
    inv_n = 1.0 / RW_HEAD_DIM
    ysum = yf_ref[0] + yb_ref[0]
    stats = _head_sums(jnp.concatenate([ysum, ysum * ysum], axis=0), bd_ref[...], 2) * inv_n
    mean = stats[:ROW_BLK]
    var = stats[ROW_BLK:] - mean * mean
    yn = (ysum - mean) * lax.rsqrt(var + GN_EPS) * gn_ref[0:1, :] + gn_ref[1:2, :]
    rw = (yn + (bonf_ref[0] + bonb_ref[0])) * g_ref[0]
    mix = _mm3(rw, wout_ref[0:RW_WIDTH, :]) + _mm3(o_ref[0], wout_ref[RW_WIDTH:, :])
    g1 = mod_ref[0, 0:1, :]
    sh2 = mod_ref[0, 1:2, :]
    sc2 = mod_ref[0, 2:3, :]
    x1 = _standardize(alpha * x_ref[0] + g1 * mix, LN_EPS) * ln_ref[0:1, :] + ln_ref[1:2, :]
    x1_ref[0] = x1
    u = _standardize(x1, LN_EPS) * (1.0 + sc2) + sh2
    u_ref[0] = u.astype(BF16)
    logits = _mm3(rt_ref[...], u, trans_b=True)
    ex = jnp.exp(logits - jnp.max(logits, axis=0, keepdims=True))
    aff_ref[0] = ex / jnp.sum(ex, axis=0, keepdims=True)


def _mix(alpha, yf, yb, bonf, bonb, g, o_mla, x, wout, gn, ln, modv, router_t, bd):
    b, t, d = x.shape
    ne = router_t.shape[0]
    row = lambda w: pl.BlockSpec((1, ROW_BLK, w), lambda bi, i: (bi, i, 0))
    full = lambda a: pl.BlockSpec(a.shape, lambda bi, i: (0,) * a.ndim)
    return pl.pallas_call(
        functools.partial(_mix_kernel, alpha),
        out_shape=[jax.ShapeDtypeStruct((b, t, d), F32),
                   jax.ShapeDtypeStruct((b, t, d), BF16),
                   jax.ShapeDtypeStruct((b, ne, t), F32)],
        grid=(b, t // ROW_BLK),
        in_specs=[row(RW_WIDTH)] * 5 + [
            row(MLA_HEADS * MLA_V_DIM), row(d), full(wout), full(gn), full(ln),
            pl.BlockSpec((1, 3, d), lambda bi, i: (bi, 0, 0)), full(router_t), full(bd)],
        out_specs=[row(d), row(d), pl.BlockSpec((1, ne, ROW_BLK), lambda bi, i: (bi, 0, i))],
        compiler_params=_cparams(("arbitrary", "arbitrary")),
        name="mix",
    )(yf, yb, bonf, bonb, g, o_mla, x, wout, gn, ln, modv, router_t, bd)


def _topk_kernel(cap, aff_ref, w_ref, pos_ref, cnt_ref):
    aff = aff_ref[0]
    ne, t = aff.shape
    bits = lax.bitcast_convert_type(aff, jnp.int32)

    def count_ge(thr):
        return jnp.sum(jnp.where(bits >= thr, 1.0, 0.0), axis=1, keepdims=True)

    def body(it, thr):
        cand = thr | lax.shift_left(jnp.int32(1), 30 - it)
        return jnp.where(count_ge(cand) >= cap, cand, thr)

    thr = lax.fori_loop(0, 31, body, jnp.zeros((ne, 1), jnp.int32))
    need = cap - jnp.sum(jnp.where(bits > thr, 1.0, 0.0), axis=1, keepdims=True)
    ri = lax.broadcasted_iota(jnp.int32, (LANES, LANES), 0)
    ci = lax.broadcasted_iota(jnp.int32, (LANES, LANES), 1)
    below = jnp.where(ri < ci, 1.0, 0.0).astype(BF16)
    ties = jnp.zeros((ne, 1), F32)
    taken = jnp.zeros((ne, 1), F32)
    for j in range(t // LANES):
        sl = slice(j * LANES, (j + 1) * LANES)
        bj = bits[:, sl]
        eqj = bj == thr
        eqf = jnp.where(eqj, 1.0, 0.0)
        before = pl.dot(eqf.astype(BF16), below) + ties
        take = jnp.logical_or(bj > thr, jnp.logical_and(eqj, before < need))
        takef = jnp.where(take, 1.0, 0.0)
        w_ref[0, :, sl] = jnp.where(take, aff[:, sl], 0.0)
        pos_ref[0, :, sl] = jnp.where(take, pl.dot(takef.astype(BF16), below) + taken, -1.0)
        cnt_ref[0, :, j:j + 1] = taken
        ties = ties + jnp.sum(eqf, axis=1, keepdims=True)
        taken = taken + jnp.sum(takef, axis=1, keepdims=True)


def _topk(aff, cap):
    b, ne, t = aff.shape
    spec = pl.BlockSpec((1, ne, t), lambda bi: (bi, 0, 0))
    return pl.pallas_call(
        functools.partial(_topk_kernel, float(cap)),
        out_shape=[jax.ShapeDtypeStruct((b, ne, t), F32), jax.ShapeDtypeStruct((b, ne, t), F32),
                   jax.ShapeDtypeStruct((b, ne, t // LANES), F32)],
        grid=(b,),
        in_specs=[spec],
        out_specs=[spec, spec, pl.BlockSpec((1, ne, t // LANES), lambda bi: (bi, 0, 0))],
        compiler_params=_cparams(("arbitrary",)),
        name="topk",
    )(aff)


def _moe_ffn_kernel(cap, ws_ref, u_ref, pos_ref, wg_ref, wu_ref, wd_ref, ye_ref, xe_ref, acc_ref):
    bi, e, q = pl.program_id(0), pl.program_id(1), pl.program_id(2)
    ne, nq = pl.num_programs(1), pl.num_programs(2)
    nwin = u_ref.shape[1] // MOE_WIN
    f = wg_ref.shape[2]

    @pl.when(q == 0)
    def _():
        xe_ref[...] = jnp.zeros_like(xe_ref)

    rows = lax.broadcasted_iota(jnp.int32, (MOE_GROWS, 1), 0).astype(F32)
    for wl in range(nwin):
        start = pl.multiple_of(ws_ref[(bi * ne + e) * (nq * nwin) + q * nwin + wl] & -SUBLANES_F32, SUBLANES_F32)
        rank = pos_ref[0, pl.ds(e, 1), wl * MOE_WIN:(wl + 1) * MOE_WIN] - start.astype(F32)
        onehot = jnp.where(rank == rows, 1.0, 0.0).astype(BF16)
        xe_ref[pl.ds(start, MOE_GROWS), :] += pl.dot(onehot, u_ref[0, wl * MOE_WIN:(wl + 1) * MOE_WIN, :])

    @pl.when(q == nq - 1)
    def _():
        xb = xe_ref[0:cap, :].astype(BF16)
        for c in range(f // MOE_FCHUNK):
            cs = slice(c * MOE_FCHUNK, (c + 1) * MOE_FCHUNK)
            hg = pl.dot(xb, wg_ref[0, :, cs])
            hu = pl.dot(xb, wu_ref[0, :, cs])
            part = pl.dot((hg * _sigmoid(hg) * hu).astype(BF16), wd_ref[0, cs, :])
            if c == 0:
                acc_ref[...] = part
            else:
                acc_ref[...] += part
        ye_ref[0, 0, 0:cap, :] = acc_ref[...].astype(BF16)
        ye_ref[0, 0, cap:, :] = jnp.zeros((ye_ref.shape[2] - cap, ye_ref.shape[3]), BF16)


def _moe_ffn(ws, u, pos, wg, wu, wd, cap):
    b, t, d = u.shape
    ne, _, f = wg.shape
    seg = min(MOE_SEG, t)
    cap_pad = cap + MOE_CROWS
    grid_spec = pltpu.PrefetchScalarGridSpec(
        num_scalar_prefetch=1,
        grid=(b, ne, t // seg),
        in_specs=[pl.BlockSpec((1, seg, d), lambda bi, e, q, ws: (bi, q, 0)),
                  pl.BlockSpec((1, ne, seg), lambda bi, e, q, ws: (bi, 0, q)),
                  pl.BlockSpec((1, d, f), lambda bi, e, q, ws: (e, 0, 0)),
                  pl.BlockSpec((1, d, f), lambda bi, e, q, ws: (e, 0, 0)),
                  pl.BlockSpec((1, f, d), lambda bi, e, q, ws: (e, 0, 0))],
        out_specs=pl.BlockSpec((1, 1, cap_pad, d), lambda bi, e, q, ws: (bi, e, 0, 0)),
        scratch_shapes=[pltpu.VMEM((cap + MOE_GROWS, d), F32), pltpu.VMEM((cap, d), F32)])
    return pl.pallas_call(
        functools.partial(_moe_ffn_kernel, cap),
        out_shape=jax.ShapeDtypeStruct((b, ne, cap_pad, d), BF16),
        grid_spec=grid_spec,
        compiler_params=_cparams(("arbitrary", "arbitrary", "arbitrary")),
        name="moe_ffn",
    )(ws, u, pos, wg, wu, wd)


def _moe_combine_kernel(alpha, ne, ws_ref, ye_ref, pos_ref, wt_ref, x1_ref, g2_ref, ln_ref, o_ref, acc_ref):
    bi, qq, s = pl.program_id(0), pl.program_id(1), pl.program_id(2)
    nwin = acc_ref.shape[0] // MOE_CWIN
    total_win = pl.num_programs(1) * nwin

    @pl.when(s == 0)
    def _():
        acc_ref[...] = jnp.zeros_like(acc_ref)

    @pl.when(s < ne)
    def _():
        lane = lax.broadcasted_iota(jnp.int32, (MOE_CWIN, ne), 1)
        cols = lax.broadcasted_iota(jnp.int32, (1, MOE_CROWS), 1).astype(F32)
        for wl in range(nwin):
            ts = slice(wl * MOE_CWIN, (wl + 1) * MOE_CWIN)
            start = pl.multiple_of(ws_ref[(bi * ne + s) * total_win + qq * nwin + wl] & -SUBLANES_BF16, SUBLANES_BF16)
            rank = jnp.sum(jnp.where(lane == s, pos_ref[0, ts, :], 0.0), axis=1, keepdims=True) - start.astype(F32)
            gate = jnp.sum(jnp.where(lane == s, wt_ref[0, ts, :], 0.0), axis=1, keepdims=True)
            onehot = jnp.where(rank == cols, 1.0, 0.0).astype(BF16)
            acc_ref[ts, :] += gate * pl.dot(onehot, ye_ref[0, 0, pl.ds(start, MOE_CROWS), :])

    @pl.when(s >= ne)
    def _():
        r0 = pl.multiple_of((s - ne) * o_ref.shape[1], o_ref.shape[1])
        xr = alpha * x1_ref[0] + g2_ref[0] * acc_ref[pl.ds(r0, o_ref.shape[1]), :]
        o_ref[0] = _standardize(xr, LN_EPS) * ln_ref[0:1, :] + ln_ref[1:2, :]


def _moe_combine(alpha, ws, ye, pos_t, wt_t, x1, g2, ln):
    b, t, d = x1.shape
    ne, cap_pad = ye.shape[1], ye.shape[2]
    seg = min(MOE_SEG, t)
    fin = min(MOE_FIN, seg)
    nfin = seg // fin
    fblk = lambda bi, qq, s, ws: (bi, qq * nfin + jnp.maximum(s - ne, 0), 0)
    grid_spec = pltpu.PrefetchScalarGridSpec(
        num_scalar_prefetch=1,
        grid=(b, t // seg, ne + nfin),
        in_specs=[pl.BlockSpec((1, 1, cap_pad, d), lambda bi, qq, s, ws: (bi, jnp.minimum(s, ne - 1), 0, 0)),
                  pl.BlockSpec((1, seg, ne), lambda bi, qq, s, ws: (bi, qq, 0)),
                  pl.BlockSpec((1, seg, ne), lambda bi, qq, s, ws: (bi, qq, 0)),
                  pl.BlockSpec((1, fin, d), fblk),
                  pl.BlockSpec((1, 1, d), lambda bi, qq, s, ws: (bi, 0, 0)),
                  pl.BlockSpec((2, d), lambda bi, qq, s, ws: (0, 0))],
        out_specs=pl.BlockSpec((1, fin, d), fblk),
        scratch_shapes=[pltpu.VMEM((seg, d), F32)])
    return pl.pallas_call(
        functools.partial(_moe_combine_kernel, alpha, ne),
        out_shape=jax.ShapeDtypeStruct((b, t, d), F32),
        grid_spec=grid_spec,
        compiler_params=_cparams(("arbitrary", "arbitrary", "arbitrary")),
        name="moe_combine",
    )(ws, ye, pos_t, wt_t, x1, g2, ln)


def _rope_tables(t, tc):
    half = MLA_ROPE_DIM // 2
    inv_freq = ROPE_BASE ** (-jnp.arange(0, half, 2, dtype=F32) / half)
    pos = jnp.arange(t)
    ang_r = (pos // GRID_W).reshape(-1, 1).astype(F32) * inv_freq
    ang_c = (pos % GRID_W).reshape(-1, 1).astype(F32) * inv_freq
    cos = jnp.concatenate([jnp.cos(ang_r)] * 2 + [jnp.cos(ang_c)] * 2, axis=-1)
    sin = jnp.concatenate([-jnp.sin(ang_r), jnp.sin(ang_r), -jnp.sin(ang_c), jnp.sin(ang_c)], axis=-1)
    tabq = jnp.concatenate([jnp.tile(cos, (1, MLA_HEADS)), jnp.tile(sin, (1, MLA_HEADS))], axis=-1)
    cos_k = jnp.concatenate([jnp.ones((tc, MLA_ROPE_DIM), F32), cos], axis=0)
    sin_k = jnp.concatenate([jnp.zeros((tc, MLA_ROPE_DIM), F32), sin], axis=0)
    return tabq, jnp.concatenate([cos_k, sin_k], axis=-1)


def _pair_swap():
    q = MLA_ROPE_DIM // 4
    return np.concatenate([np.arange(q, 2 * q), np.arange(0, q), np.arange(3 * q, 4 * q), np.arange(2 * q, 3 * q)])


def kernel(x, c, ctx, c_ctx, w_ada, b_ada, w_in, rwkv_conv, rwkv_w0, rwkv_w_up, rwkv_a0, rwkv_a_up, rwkv_g_up, rwkv_k_k, rwkv_k_a, rwkv_r_k, rwkv_gn_g, rwkv_gn_b, mla_q_norm, mla_w_uq, mla_kv_norm, mla_w_uk, mla_w_uv, w_out, ln1_g, ln1_b, router, exp_w_gate, exp_w_up, exp_w_down, ln2_g, ln2_b):
    b, t, d = x.shape
    tc = ctx.shape[1]
    depth = w_ada.shape[0]
    assert depth == 1 and tc == ROW_BLK and t % ROW_BLK == 0 and t % LANES == 0
    alpha = (2.0 * depth) ** 0.25
    cap = CAPACITY_FACTOR * t // N_EXPERTS

    pad = (-(b + 1)) % 8
    cc = jnp.concatenate([c, c_ctx[None, :], jnp.zeros((pad, d), F32)], axis=0)
    mod = _ada(cc, w_ada[0], b_ada)
    sh1, sc1, g1, sh2, sc2, g2 = jnp.split(mod, 6, axis=-1)
    ss_lat = jnp.stack([sh1[:b], sc1[:b]], axis=1)
    ss_ctx = jnp.broadcast_to(jnp.stack([sh1[b], sc1[b]], axis=0)[None], (b, 2, d))
    ss = jnp.stack([ss_ctx, ss_lat], axis=1)

    swap = _pair_swap()
    kr0 = RW_PROJ + MLA_Q_LORA + MLA_KV_LORA
    w_ext = jnp.concatenate([w_in[0], w_in[0][:, kr0 + swap]], axis=1)
    hcat = _inproj(ctx, x, ss, w_ext)

    head_ones = np.kron(np.eye(RW_HEADS, dtype=np.float32), np.ones((RW_HEAD_DIM, RW_HEAD_DIM), np.float32))
    bd = jnp.asarray(head_ones, BF16)
    vecs = jnp.stack([rwkv_k_k[0], rwkv_k_a[0], rwkv_r_k[0].reshape(-1)], axis=0)
    yf, yb, bonf, bonb, gate = _rwkv(hcat, rwkv_conv[0], rwkv_w0[0][:, None, :], rwkv_w_up[0],
                                     rwkv_a0[0][:, None, :], rwkv_a_up[0], rwkv_g_up[0], vecs,
                                     bd[:GROUP_LANES, :GROUP_LANES])

    wq = mla_w_uq[0].reshape(MLA_Q_LORA, MLA_HEADS, MLA_QK_DIM)
    wq_nope = wq[:, :, :MLA_NOPE_DIM].reshape(MLA_Q_LORA, -1)
    wq_rope = wq[:, :, MLA_NOPE_DIM:]
    wq_ext = jnp.concatenate([wq_nope, wq_rope.reshape(MLA_Q_LORA, -1),
                              wq_rope[:, :, swap].reshape(MLA_Q_LORA, -1)], axis=1)
    wkv = jnp.concatenate([mla_w_uk[0], mla_w_uv[0]], axis=1)
    tabq, tabk = _rope_tables(t, tc)
    q, k, v = _mla_qkv(hcat, tabq, tabk, mla_q_norm, mla_kv_norm, wq_ext, wkv)
    o_mla = _attention(q, k, v)

    gn = jnp.stack([rwkv_gn_g[0], rwkv_gn_b[0]], axis=0)
    ln1 = jnp.stack([ln1_g[0], ln1_b[0]], axis=0)
    modv = jnp.stack([g1[:b], sh2[:b], sc2[:b]], axis=1)
    x1, u, aff = _mix(alpha, yf, yb, bonf, bonb, gate, o_mla, x, w_out[0], gn, ln1, modv, router[0].T, bd[:GROUP_LANES, :GROUP_LANES])

    wt, pos, cnt = _topk(aff, cap)
    cnt = cnt.astype(jnp.int32)
    ws = cnt[:, :, ::MOE_WIN // LANES].reshape(-1)
    ws_c = cnt[:, :, ::MOE_CWIN // LANES].reshape(-1)
    ye = _moe_ffn(ws, u, pos, exp_w_gate[0].astype(BF16), exp_w_up[0].astype(BF16), exp_w_down[0].astype(BF16), cap)
    ln2 = jnp.stack([ln2_g[0], ln2_b[0]], axis=0)
    return _moe_combine(alpha, ws_c, ye, jnp.swapaxes(pos, 1, 2), jnp.swapaxes(wt, 1, 2), x1, g2[:b, None, :], ln2)
```
